```python
import jax
import jax.numpy as jnp
from jax import lax
import numpy as np

D_MODEL = 1024
BATCH = 4
SEQ = 4096
DEPTH = 4
DEC_BATCH = 16
DEC_SEQ = 64
PAST_LEN = 4096

CHUNK = 64
N_META = 16
Q_BLOCK = 128
N_MIXERS = 3
N_LAYERS_A = (DEPTH + 2) // 3
N_LAYERS_B = (DEPTH + 1) // 3
N_LAYERS_C = DEPTH // 3
N_HEADS_A = 16
HEAD_DIM_A = 64
N_HEADS_B = 16
HEAD_DIM_B = 64
N_HEADS_C = 16
Q_LORA = 384
KV_LORA = 256
QK_NOPE = 64
QK_ROPE = 32
V_DIM_C = 64
ROPE_THETA = 10000.0
N_EXPERTS = 32
TOP_K = 4
D_EXPERT = 1024
SWIGLU_LIMIT = 7.0
SWIGLU_ALPHA = 1.702
DN_ALPHA = (2 * DEPTH) ** 0.25
DN_BETA = (8 * DEPTH) ** -0.25
LN_EPS = 1e-5
RMS_EPS = 1e-6

kernel_name = 'stickbreak_fox_mla_moe_stream_step'


def _layer_norm(x, g, b):
    xf = x.astype(jnp.float32)
    mu = jnp.mean(xf, -1, keepdims=True)
    var = jnp.mean(jnp.square(xf - mu), -1, keepdims=True)
    return ((xf - mu) * lax.rsqrt(var + LN_EPS) * g.astype(jnp.float32) + b.astype(jnp.float32)).astype(x.dtype)


def _rms_norm(x, g):
    xf = x.astype(jnp.float32)
    return (xf * lax.rsqrt(jnp.mean(jnp.square(xf), -1, keepdims=True) + RMS_EPS) * g.astype(jnp.float32)).astype(x.dtype)


def _rope(x, pos):
    half = x.shape[-1] // 2
    inv = ROPE_THETA ** (-jnp.arange(half, dtype=jnp.float32) / half)
    ang = pos.astype(jnp.float32)[:, None] * inv
    ang = ang.reshape((ang.shape[0],) + (1,) * (x.ndim - 3) + (half,))
    cos, sin = jnp.cos(ang), jnp.sin(ang)
    xf = x.astype(jnp.float32)
    x1, x2 = xf[..., :half], xf[..., half:]
    return jnp.concatenate([x1 * cos - x2 * sin, x1 * sin + x2 * cos], -1).astype(x.dtype)


def _chunk_id(pos):
    return (pos - N_META) // CHUNK


def _sweep(attend, q_args, q_pos):
    Lq = q_pos.shape[0]
    qb = min(Q_BLOCK, Lq)
    nb = -(-Lq // qb)
    pad = nb * qb - Lq

    def to_blocks(a):
        a = jnp.pad(a, [(0, 0), (0, pad)] + [(0, 0)] * (a.ndim - 2))
        return jnp.moveaxis(a.reshape((a.shape[0], nb, qb) + a.shape[2:]), 1, 0)

    blocks = tuple(to_blocks(a) for a in q_args)
    qp = jnp.pad(q_pos, (0, pad), mode='edge').reshape(nb, qb)
    out = lax.map(lambda xs: attend(*xs[0], xs[1]), (blocks, qp))
    out = jnp.moveaxis(out, 0, 1)
    return out.reshape((out.shape[0], nb * qb) + out.shape[3:])[:, :Lq]


def _stick_breaking(h, q_pos, k_pos, past_k, past_v, w_qkv, w_o):
    B, L, _ = h.shape
    qkv = (h @ w_qkv).reshape(B, L, 3, N_HEADS_A, HEAD_DIM_A)
    q, k, v = qkv[:, :, 0], qkv[:, :, 1], qkv[:, :, 2]
    kk = jnp.concatenate([past_k, k], 1)
    vv = jnp.concatenate([past_v, v], 1)
    scale = HEAD_DIM_A ** -0.5

    def attend(qb, qp):
        z = jnp.einsum('bqhd,bkhd->bhqk', qb, kk, preferred_element_type=jnp.float32) * scale
        mask = k_pos[None, :] < qp[:, None]
        l = jnp.where(mask, jax.nn.log_sigmoid(-z), 0.0)
        c = jnp.cumsum(l, axis=-1)
        a = jnp.where(mask, jnp.exp(jax.nn.log_sigmoid(z) + c[..., -1:] - c), 0.0)
        return jnp.einsum('bhqk,bkhd->bqhd', a.astype(vv.dtype), vv)

    o = _sweep(attend, (q,), q_pos)
    return o.reshape(B, L, -1) @ w_o, k, v


def _forgetting(h, q_pos, k_pos, past_k, past_v, past_lf, w_qkv, w_f, b_f, w_o):
    B, L, _ = h.shape
    qkv = (h @ w_qkv).reshape(B, L, 3, N_HEADS_B, HEAD_DIM_B)
    q, k, v = qkv[:, :, 0], qkv[:, :, 1], qkv[:, :, 2]
    lf = jax.nn.log_sigmoid((h @ w_f + b_f).astype(jnp.float32))
    kk = jnp.concatenate([past_k, k], 1)
    vv = jnp.concatenate([past_v, v], 1)
    cum = jnp.cumsum(jnp.concatenate([past_lf.astype(jnp.float32), lf], 1), axis=1)
    cum_k = jnp.transpose(cum, (0, 2, 1))[:, :, None, :]
    scale = HEAD_DIM_B ** -0.5

    def attend(qb, cq, qp):
        s = (jnp.einsum('bqhd,bkhd->bhqk', qb, kk, preferred_element_type=jnp.float32) * scale
             + jnp.transpose(cq, (0, 2, 1))[..., None] - cum_k)
        mask = k_pos[None, :] <= qp[:, None]
        p = jax.nn.softmax(jnp.where(mask, s, -jnp.inf), axis=-1)
        return jnp.einsum('bhqk,bkhd->bqhd', p.astype(vv.dtype), vv)

    o = _sweep(attend, (q, cum[:, -L:]), q_pos)
    return o.reshape(B, L, -1) @ w_o, k, v, lf.astype(h.dtype)


def _mla(h, q_pos, k_pos, past_c, past_kr, w_dq, g_q, w_uq, w_dkv, g_kv, w_uk, w_uv, w_o):
    B, L, _ = h.shape
    q = (_rms_norm(h @ w_dq, g_q) @ w_uq).reshape(B, L, N_HEADS_C, QK_NOPE + QK_ROPE)
    q_nope, q_rope = q[..., :QK_NOPE], _rope(q[..., QK_NOPE:], q_pos)
    kv = h @ w_dkv
    c = _rms_norm(kv[..., :KV_LORA], g_kv)
    kr = _rope(kv[..., KV_LORA:], q_pos)
    cc = jnp.concatenate([past_c, c], 1)
    kkr = jnp.concatenate([past_kr, kr], 1)
    q_lat = jnp.einsum('bqhn,chn->bqhc', q_nope, w_uk)
    scale = (QK_NOPE + QK_ROPE) ** -0.5
    k_chunk = _chunk_id(k_pos)

    def attend(ql, qr, qp):
        s = (jnp.einsum('bqhc,bkc->bhqk', ql, cc, preferred_element_type=jnp.float32)
             + jnp.einsum('bqhr,bkr->bhqk', qr, kkr, preferred_element_type=jnp.float32)) * scale
        mask = k_chunk[None, :] <= _chunk_id(qp)[:, None]
        p = jax.nn.softmax(jnp.where(mask, s, -jnp.inf), axis=-1)
        return jnp.einsum('bhqk,bkc->bqhc', p.astype(cc.dtype), cc)

    o_lat = _sweep(attend, (q_lat, q_rope), q_pos)
    o = jnp.einsum('bqhc,chv->bqhv', o_lat, w_uv)
    return o.reshape(B, L, -1) @ w_o, c, kr


def _moe(h, w_r, b_r, w_gu, b_gu, w_dn, b_dn):
    B, L, D = h.shape
    x = h.reshape(B * L, D)
    logits = jnp.matmul(x, w_r, preferred_element_type=jnp.float32) + b_r.astype(jnp.float32)
    top_v, top_i = lax.top_k(logits, TOP_K)
    gates = jax.nn.softmax(top_v, axis=-1)
    comb = jnp.sum(jax.nn.one_hot(top_i, N_EXPERTS, dtype=jnp.float32) * gates[..., None], axis=1)

    def expert_step(acc, p):
        wg, bg, wd, bd, ce = p
        gu = x @ wg + bg
        gate = jnp.minimum(gu[:, :D_EXPERT], SWIGLU_LIMIT)
        up = jnp.clip(gu[:, D_EXPERT:], -SWIGLU_LIMIT, SWIGLU_LIMIT)
        act = (up + 1.0) * gate * jax.nn.sigmoid(gate * SWIGLU_ALPHA)
        return acc + ce[:, None] * (act @ wd + bd), None

    acc, _ = lax.scan(expert_step, jnp.zeros(x.shape, jnp.float32), (w_gu, b_gu, w_dn, b_dn, comb.T))
    return acc.astype(h.dtype).reshape(B, L, D)


def _trunk(h, past, W):
    sb_k, sb_v, fx_k, fx_v, fx_lf, ml_c, ml_kr = past
    P = sb_k.shape[2]
    L = h.shape[1]
    k_pos = jnp.arange(P + L, dtype=jnp.int32)
    q_pos = k_pos[P:]
    new = ([], [], [], [], [], [], [])
    for i in range(DEPTH):
        kind, j = i % N_MIXERS, i // N_MIXERS
        if kind == 0:
            y, k, v = _stick_breaking(h, q_pos, k_pos, sb_k[j], sb_v[j], W['a_w_qkv'][j], W['a_w_o'][j])
            new[0].append(k)
            new[1].append(v)
        elif kind == 1:
            y, k, v, lf = _forgetting(h, q_pos, k_pos, fx_k[j], fx_v[j], fx_lf[j], W['b_w_qkv'][j],
                                      W['b_w_f'][j], W['b_b_f'][j], W['b_w_o'][j])
            new[2].append(k)
            new[3].append(v)
            new[4].append(lf)
        else:
            y, c, kr = _mla(h, q_pos, k_pos, ml_c[j], ml_kr[j], W['c_w_dq'][j], W['c_g_q'][j], W['c_w_uq'][j],
                            W['c_w_dkv'][j], W['c_g_kv'][j], W['c_w_uk'][j], W['c_w_uv'][j], W['c_w_o'][j])
            new[5].append(c)
            new[6].append(kr)
        h = _layer_norm(DN_ALPHA * h + y, W['ln1_g'][i], W['ln1_b'][i])
        f = _moe(h, W['moe_w_router'][i], W['moe_b_router'][i], W['moe_w_gate_up'][i], W['moe_b_gate_up'][i],
                 W['moe_w_down'][i], W['moe_b_down'][i])
        h = _layer_norm(DN_ALPHA * h + f, W['ln2_g'][i], W['ln2_b'][i])
    return h, [jnp.stack(rows) for rows in new]


def setup_inputs(seed: int = 0) -> dict:
    key = jax.random.key(seed)
    ks = iter(jax.random.split(key, 48))
    f32 = jnp.float32
    R = N_META + PAST_LEN
    D = D_MODEL
    HA, DA, HB, DB, HC = N_HEADS_A, HEAD_DIM_A, N_HEADS_B, HEAD_DIM_B, N_HEADS_C

    def nrm(shape, s=1.0):
        return jax.random.normal(next(ks), shape, f32) * s

    def unif(shape, lo, hi):
        return jax.random.uniform(next(ks), shape, f32, lo, hi)

    v_scale_a = jnp.concatenate([jnp.ones((2 * HA * DA,), f32), jnp.full((HA * DA,), DN_BETA, f32)])
    v_scale_b = jnp.concatenate([jnp.ones((2 * HB * DB,), f32), jnp.full((HB * DB,), DN_BETA, f32)])
    inp = {}
    inp['x_prompt'] = nrm((BATCH, SEQ, D))
    inp['x_sample'] = nrm((DEC_BATCH, DEC_SEQ, D))
    inp['cache_sb_k'] = nrm((N_LAYERS_A, DEC_BATCH, R, HA, DA))
    inp['cache_sb_v'] = nrm((N_LAYERS_A, DEC_BATCH, R, HA, DA), DN_BETA)
    inp['cache_fox_k'] = nrm((N_LAYERS_B, DEC_BATCH, R, HB, DB))
    inp['cache_fox_v'] = nrm((N_LAYERS_B, DEC_BATCH, R, HB, DB), DN_BETA)
    inp['cache_fox_logf'] = jax.nn.log_sigmoid(unif((N_LAYERS_B, DEC_BATCH, R, HB), 1.0, 4.0))
    inp['cache_mla_ckv'] = nrm((N_LAYERS_C, DEC_BATCH, R, KV_LORA))
    inp['cache_mla_krope'] = nrm((N_LAYERS_C, DEC_BATCH, R, QK_ROPE))
    inp['meta_tokens'] = nrm((N_META, D))
    inp['a_w_qkv'] = nrm((N_LAYERS_A, D, 3 * HA * DA), D ** -0.5) * v_scale_a
    inp['a_w_o'] = nrm((N_LAYERS_A, HA * DA, D), DN_BETA * (HA * DA) ** -0.5)
    inp['b_w_qkv'] = nrm((N_LAYERS_B, D, 3 * HB * DB), D ** -0.5) * v_scale_b
    inp['b_w_f'] = nrm((N_LAYERS_B, D, HB), D ** -0.5)
    inp['b_b_f'] = unif((N_LAYERS_B, HB), 1.0, 4.0)
    inp['b_w_o'] = nrm((N_LAYERS_B, HB * DB, D), DN_BETA * (HB * DB) ** -0.5)
    inp['c_w_dq'] = nrm((N_LAYERS_C, D, Q_LORA), D ** -0.5)
    inp['c_g_q'] = 1.0 + nrm((N_LAYERS_C, Q_LORA), 0.02)
    inp['c_w_uq'] = nrm((N_LAYERS_C, Q_LORA, HC * (QK_NOPE + QK_ROPE)), Q_LORA ** -0.5)
    inp['c_w_dkv'] = nrm((N_LAYERS_C, D, KV_LORA + QK_ROPE), D ** -0.5)
    inp['c_g_kv'] = 1.0 + nrm((N_LAYERS_C, KV_LORA), 0.02)
    inp['c_w_uk'] = nrm((N_LAYERS_C, KV_LORA, HC, QK_NOPE), KV_LORA ** -0.5)
    inp['c_w_uv'] = nrm((N_LAYERS_C, KV_LORA, HC, V_DIM_C), DN_BETA * KV_LORA ** -0.5)
    inp['c_w_o'] = nrm((N_LAYERS_C, HC * V_DIM_C, D), DN_BETA * (HC * V_DIM_C) ** -0.5)
    inp['ln1_g'] = 1.0 + nrm((DEPTH, D), 0.02)
    inp['ln1_b'] = nrm((DEPTH, D), 0.02)
    inp['ln2_g'] = 1.0 + nrm((DEPTH, D), 0.02)
    inp['ln2_b'] = nrm((DEPTH, D), 0.02)
    inp['moe_w_router'] = nrm((DEPTH, D, N_EXPERTS), D ** -0.5)
    inp['moe_b_router'] = nrm((DEPTH, N_EXPERTS), 0.01)
    inp['moe_w_gate_up'] = nrm((DEPTH, N_EXPERTS, D, 2 * D_EXPERT), DN_BETA * D ** -0.5)
    inp['moe_b_gate_up'] = nrm((DEPTH, N_EXPERTS, 2 * D_EXPERT), 0.02)
    inp['moe_w_down'] = nrm((DEPTH, N_EXPERTS, D_EXPERT, D), DN_BETA * D_EXPERT ** -0.5)
    inp['moe_b_down'] = nrm((DEPTH, N_EXPERTS, D), 0.02)
    return inp


def reference(x_prompt, x_sample, cache_sb_k, cache_sb_v, cache_fox_k, cache_fox_v, cache_fox_logf,
              cache_mla_ckv, cache_mla_krope, meta_tokens, a_w_qkv, a_w_o, b_w_qkv, b_w_f, b_b_f, b_w_o,
              c_w_dq, c_g_q, c_w_uq, c_w_dkv, c_g_kv, c_w_uk, c_w_uv, c_w_o, ln1_g, ln1_b, ln2_g, ln2_b,
              moe_w_router, moe_b_router, moe_w_gate_up, moe_b_gate_up, moe_w_down, moe_b_down):
    W = {'a_w_qkv': a_w_qkv, 'a_w_o': a_w_o, 'b_w_qkv': b_w_qkv, 'b_w_f': b_w_f, 'b_b_f': b_b_f,
         'b_w_o': b_w_o, 'c_w_dq': c_w_dq, 'c_g_q': c_g_q, 'c_w_uq': c_w_uq, 'c_w_dkv': c_w_dkv,
         'c_g_kv': c_g_kv, 'c_w_uk': c_w_uk, 'c_w_uv': c_w_uv, 'c_w_o': c_w_o, 'ln1_g': ln1_g,
         'ln1_b': ln1_b, 'ln2_g': ln2_g, 'ln2_b': ln2_b, 'moe_w_router': moe_w_router,
         'moe_b_router': moe_b_router, 'moe_w_gate_up': moe_w_gate_up, 'moe_b_gate_up': moe_b_gate_up,
         'moe_w_down': moe_w_down, 'moe_b_down': moe_b_down}
    B = x_prompt.shape[0]
    dt = x_prompt.dtype

    def empty(n, *tail):
        return jnp.zeros((n, B, 0) + tail, dt)

    past_prompt = (empty(N_LAYERS_A, N_HEADS_A, HEAD_DIM_A), empty(N_LAYERS_A, N_HEADS_A, HEAD_DIM_A),
                   empty(N_LAYERS_B, N_HEADS_B, HEAD_DIM_B), empty(N_LAYERS_B, N_HEADS_B, HEAD_DIM_B),
                   empty(N_LAYERS_B, N_HEADS_B), empty(N_LAYERS_C, KV_LORA), empty(N_LAYERS_C, QK_ROPE))
    meta = jnp.broadcast_to(meta_tokens.astype(dt)[None], (B, N_META, D_MODEL))
    h_p, new_p = _trunk(jnp.concatenate([meta, x_prompt], 1), past_prompt, W)
    y_prompt = h_p[:, N_META:]
    past_sample = (cache_sb_k, cache_sb_v, cache_fox_k, cache_fox_v, cache_fox_logf, cache_mla_ckv, cache_mla_krope)
    y_sample, new_s = _trunk(x_sample, past_sample, W)
    return (y_prompt, y_sample,
            new_p[0], new_p[1], new_p[2], new_p[3], new_p[4], new_p[5], new_p[6],
            new_s[0], new_s[1], new_s[2], new_s[3], new_s[4], new_s[5], new_s[6])
```

```python
import functools

import jax
import jax.numpy as jnp
from jax import lax
from jax.experimental import pallas as pl
from jax.experimental.pallas import tpu as pltpu

F32 = jnp.float32
BF16 = jnp.bfloat16

D_MODEL = 1024
N_META = 16
CHUNK = 64
N_HEADS = 16
HEAD_DIM = 64
Q_LORA = 384
KV_LORA = 256
QK_ROPE = 32
ROPE_THETA = 10000.0
N_EXPERTS = 32
TOP_K = 4
D_EXPERT = 1024
SWIGLU_LIMIT = 7.0
SWIGLU_ALPHA = 1.702
DEPTH = 4
DN_ALPHA = (2 * DEPTH) ** 0.25
LN_EPS = 1e-5
RMS_EPS = 1e-6

LANES = 128
KV_TILE = 256
SMALL_TILE = 128
NEG = -1e30
VMEM_LIMIT = 56 * 2 ** 20


def _cparams(sem, vmem=None):
    return pltpu.CompilerParams(dimension_semantics=sem, vmem_limit_bytes=vmem)


def _dot(a, b):
    return jnp.dot(a, b, preferred_element_type=F32)


def _dot_nt(a, b):
    return lax.dot_general(a, b, (((1,), (1,)), ((), ())), preferred_element_type=F32)


def _split3(x):
    hi = x.astype(BF16)
    r = x - hi.astype(F32)
    mid = r.astype(BF16)
    lo = (r - mid.astype(F32)).astype(BF16)
    return hi, mid, lo


def _iota(shape, axis):
    return lax.broadcasted_iota(jnp.int32, shape, axis)


def _tri(n, op):
    r, c = _iota((n, n), 0), _iota((n, n), 1)
    m = {"gt": r > c, "ge": r >= c, "le": r <= c}[op]
    return jnp.where(m, 1.0, 0.0).astype(BF16)


def _pad_rows(x, n):
    if x.shape[0] == n:
        return x
    return jnp.concatenate([x, jnp.zeros((n - x.shape[0],) + x.shape[1:], x.dtype)], axis=0)


def _log_sigmoid(x):
    return jnp.minimum(x, 0.0) - jnp.log1p(jnp.exp(-jnp.abs(x)))


def _mm_kernel(x_ref, w_ref, o_ref):
    o_ref[...] = _dot(x_ref[...], w_ref[...]).astype(o_ref.dtype)


def _matmul(x, w, *, name, tm=512, tn=None, out_dtype=F32):
    M, K = x.shape
    N = w.shape[1]
    tn = N if tn is None else tn
    return pl.pallas_call(
        _mm_kernel,
        grid=(N // tn, pl.cdiv(M, tm)),
        in_specs=[pl.BlockSpec((tm, K), lambda j, i: (i, 0)),
                  pl.BlockSpec((K, tn), lambda j, i: (0, j))],
        out_specs=pl.BlockSpec((tm, tn), lambda j, i: (i, j)),
        out_shape=jax.ShapeDtypeStruct((M, N), out_dtype),
        compiler_params=_cparams(("arbitrary", "arbitrary")),
        name=name,
    )(x, w)


def _ln_kernel(h_ref, y_ref, g_ref, b_ref, o_ref, ob_ref):
    x = DN_ALPHA * h_ref[...] + y_ref[...]
    mu = jnp.mean(x, axis=-1, keepdims=True)
    xc = x - mu
    var = jnp.mean(xc * xc, axis=-1, keepdims=True)
    o = xc * lax.rsqrt(var + LN_EPS) * g_ref[...] + b_ref[...]
    o_ref[...] = o
    ob_ref[...] = o.astype(BF16)


def _deepnorm_ln(h, y, g, b, *, name, tm=512):
    M, D = h.shape
    row = pl.BlockSpec((tm, D), lambda i: (i, 0))
    vec = pl.BlockSpec((1, D), lambda i: (0, 0))
    return pl.pallas_call(
        _ln_kernel,
        grid=(pl.cdiv(M, tm),),
        in_specs=[row, row, vec, vec],
        out_specs=[row, row],
        out_shape=[jax.ShapeDtypeStruct((M, D), F32), jax.ShapeDtypeStruct((M, D), BF16)],
        compiler_params=_cparams(("arbitrary",)),
        name=name,
    )(h, y, g.reshape(1, D), b.reshape(1, D))


def _sb_tile(qm, kb, vb, r, acc, mask, tri_gt):
    s = _dot_nt(qm, kb)
    ls = _log_sigmoid(s)
    l = ls - s
    if mask is not None:
        l = jnp.where(mask, l, 0.0)
    lhi = l.astype(BF16)
    llo = (l - lhi.astype(F32)).astype(BF16)
    c = _dot(lhi, tri_gt) + _dot(llo, tri_gt)
    a = jnp.exp(ls + c + r)
    if mask is not None:
        a = jnp.where(mask, a, 0.0)
    acc = acc + _dot(a.astype(BF16), vb)
    r = r + jnp.sum(l, axis=-1, keepdims=True)
    return r, acc


def _head_masks(q):
    lane = _iota(q.shape, 1)
    q0 = jnp.where(lane < HEAD_DIM, q, 0.0).astype(BF16)
    q1 = jnp.where(lane >= HEAD_DIM, q, 0.0).astype(BF16)
    return q0, q1


def _merge_heads(o0, o1):
    lane = _iota(o0.shape, 1)
    return jnp.where(lane < HEAD_DIM, o0, o1)


def _sb_prompt_kernel(q_ref, k_ref, v_ref, o_ref, *, seq):
    nfull = seq // KV_TILE
    tail = seq - nfull * KV_TILE
    tri_full = _tri(KV_TILE, "gt")
    tri_small = _tri(SMALL_TILE, "gt")
    scale = HEAD_DIM ** -0.5

    def kv(ks, n):
        return k_ref[pl.ds(ks, n), :].astype(BF16), v_ref[pl.ds(ks, n), :].astype(BF16)

    def q_block(qs, tq, n_left, diag_n):
        q = q_ref[pl.ds(qs, tq), :] * scale
        outs = []
        for qm in _head_masks(q):
            kb, vb = kv(qs, diag_n)
            width = KV_TILE if diag_n == KV_TILE else SMALL_TILE
            kb, vb = _pad_rows(kb, width), _pad_rows(vb, width)
            row, col = _iota((tq, width), 0), _iota((tq, width), 1)
            mask = col < row
            r = jnp.zeros((tq, 1), F32)
            acc = jnp.zeros((tq, LANES), F32)
            r, acc = _sb_tile(qm, kb, vb, r, acc, mask, tri_full if width == KV_TILE else tri_small)

            def body(jj, carry, qm=qm):
                ks = pl.multiple_of((n_left - 1 - jj) * KV_TILE, KV_TILE)
                kb, vb = kv(ks, KV_TILE)
                return _sb_tile(qm, kb, vb, carry[0], carry[1], None, tri_full)

            r, acc = lax.fori_loop(0, n_left, body, (r, acc))
            outs.append(acc)
        o_ref[pl.ds(qs, tq), :] = _merge_heads(*outs).astype(o_ref.dtype)

    def full_block(i, carry):
        q_block(pl.multiple_of(i * KV_TILE, KV_TILE), KV_TILE, i, KV_TILE)
        return carry

    lax.fori_loop(0, nfull, full_block, 0)
    if tail:
        q_block(nfull * KV_TILE, tail, nfull, tail)


def _sb_sample_kernel(q_ref, kn_ref, vn_ref, kc_ref, vc_ref, o_ref, *, past):
    nfull = past // KV_TILE
    tail = past - nfull * KV_TILE
    tq = q_ref.shape[0]
    tri_full = _tri(KV_TILE, "gt")
    tri_small = _tri(SMALL_TILE, "gt")
    q = q_ref[...] * (HEAD_DIM ** -0.5)
    row, col = _iota((tq, SMALL_TILE), 0), _iota((tq, SMALL_TILE), 1)
    outs = []
    for qm in _head_masks(q):
        r = jnp.zeros((tq, 1), F32)
        acc = jnp.zeros((tq, LANES), F32)
        kb = _pad_rows(kn_ref[...].astype(BF16), SMALL_TILE)
        vb = _pad_rows(vn_ref[...].astype(BF16), SMALL_TILE)
        r, acc = _sb_tile(qm, kb, vb, r, acc, col < row, tri_small)
        if tail:
            kb = _pad_rows(kc_ref[0, pl.ds(nfull * KV_TILE, tail), :].astype(BF16), SMALL_TILE)
            vb = _pad_rows(vc_ref[0, pl.ds(nfull * KV_TILE, tail), :].astype(BF16), SMALL_TILE)
            r, acc = _sb_tile(qm, kb, vb, r, acc, col < tail, tri_small)

        def body(jj, carry, qm=qm):
            ks = pl.multiple_of((nfull - 1 - jj) * KV_TILE, KV_TILE)
            kb = kc_ref[0, pl.ds(ks, KV_TILE), :].astype(BF16)
            vb = vc_ref[0, pl.ds(ks, KV_TILE), :].astype(BF16)
            return _sb_tile(qm, kb, vb, carry[0], carry[1], None, tri_full)

        r, acc = lax.fori_loop(0, nfull, body, (r, acc))
        outs.append(acc)
    o_ref[...] = _merge_heads(*outs).astype(o_ref.dtype)


def _qkv_specs(seq, row_block0, n_col_blocks):
    def spec(part):
        return pl.BlockSpec((seq, LANES), lambda b, hp: (row_block0 + b, part * n_col_blocks + hp))
    return spec(0), spec(1), spec(2)


def _sb_attention(qkv, cache_k, cache_v, dims):
    bp, lp, bs, ls, past = dims
    hp = N_HEADS // 2
    o_p = pl.pallas_call(
        functools.partial(_sb_prompt_kernel, seq=lp),
        grid=(bp, hp),
        in_specs=list(_qkv_specs(lp, 0, hp)),
        out_specs=pl.BlockSpec((lp, LANES), lambda b, h: (b, h)),
        out_shape=jax.ShapeDtypeStruct((bp * lp, N_HEADS * HEAD_DIM), BF16),
        compiler_params=_cparams(("arbitrary", "arbitrary"), VMEM_LIMIT),
        name="sb_attn_prompt",
    )(qkv, qkv, qkv)
    cache_spec = pl.BlockSpec((1, past, LANES), lambda b, h: (b, 0, h))
    o_s = pl.pallas_call(
        functools.partial(_sb_sample_kernel, past=past),
        grid=(bs, hp),
        in_specs=list(_qkv_specs(ls, bp * lp // ls, hp)) + [cache_spec, cache_spec],
        out_specs=pl.BlockSpec((ls, LANES), lambda b, h: (b, h)),
        out_shape=jax.ShapeDtypeStruct((bs * ls, N_HEADS * HEAD_DIM), BF16),
        compiler_params=_cparams(("arbitrary", "arbitrary"), VMEM_LIMIT),
        name="sb_attn_sample",
    )(qkv, qkv, qkv, cache_k, cache_v)
    return jnp.concatenate([o_p, o_s], axis=0)


def _cumsum_rows(x, tri_ge):
    hi, mid, lo = _split3(x)
    return _dot(tri_ge, hi) + _dot(tri_ge, mid) + _dot(tri_ge, lo)


def _cumsum_lanes(x, tri_le):
    hi, mid, lo = _split3(x)
    return _dot(hi, tri_le) + _dot(mid, tri_le) + _dot(lo, tri_le)


def _gate_chunk(hc, valid, wf_ref, wft_ref, bf_row_ref, bf_col_ref):
    n = hc.shape[0]
    g = _dot(hc, wf_ref[...]) + bf_row_ref[...]
    gt = _dot_nt(wft_ref[...], hc) + bf_col_ref[...]
    lf, lft = _log_sigmoid(g), _log_sigmoid(gt)
    if valid < n:
        lf = jnp.where(_iota(lf.shape, 0) < valid, lf, 0.0)
        lft = jnp.where(_iota(lft.shape, 1) < valid, lft, 0.0)
    return lf, lft


def _fox_gate_prompt_kernel(h_ref, wf_ref, wft_ref, bfr_ref, bfc_ref, lf_ref, cum_ref, cumt_ref, *, seq):
    nfull = seq // KV_TILE
    tail = seq - nfull * KV_TILE
    carry = jnp.zeros((1, LANES), F32)
    carry_t = jnp.zeros((N_HEADS, 1), F32)
    for j in range(nfull + (1 if tail else 0)):
        n = KV_TILE if j < nfull else SMALL_TILE
        valid = KV_TILE if j < nfull else tail
        hc = _pad_rows(h_ref[j * KV_TILE:j * KV_TILE + valid, :], n)
        lf, lft = _gate_chunk(hc, valid, wf_ref, wft_ref, bfr_ref, bfc_ref)
        cum = _cumsum_rows(lf, _tri(n, "ge")) + carry
        cum_t = _cumsum_lanes(lft, _tri(n, "le")) + carry_t
        lf_ref[j * KV_TILE:j * KV_TILE + valid, :] = lf[:valid]
        cum_ref[j * KV_TILE:j * KV_TILE + valid, :] = cum[:valid]
        cumt_ref[0, j] = _pad_lanes(cum_t, KV_TILE)
        carry = cum[valid - 1:valid, :]
        carry_t = cum_t[:, valid - 1:valid]


def _pad_lanes(x, n):
    if x.shape[1] == n:
        return x
    return jnp.concatenate([x, jnp.zeros((x.shape[0], n - x.shape[1]), x.dtype)], axis=1)


def _fox_gate_sample_kernel(h_ref, plft_ref, wf_ref, wft_ref, bfr_ref, bfc_ref,
                            lf_ref, cum_ref, cumt_past_ref, cumt_new_ref):
    ntiles = plft_ref.shape[2] // KV_TILE
    carry_t = jnp.zeros((N_HEADS, 1), F32)
    tri_le = _tri(KV_TILE, "le")
    for j in range(ntiles):
        cum_t = _cumsum_lanes(plft_ref[0, :, j * KV_TILE:(j + 1) * KV_TILE], tri_le) + carry_t
        cumt_past_ref[0, j] = cum_t
        carry_t = cum_t[:, KV_TILE - 1:KV_TILE]
    ls = h_ref.shape[0]
    hc = _pad_rows(h_ref[...], SMALL_TILE)
    lf, lft = _gate_chunk(hc, ls, wf_ref, wft_ref, bfr_ref, bfc_ref)
    eye = _iota((N_HEADS, LANES), 0) == _iota((N_HEADS, LANES), 1)
    carry = jnp.sum(jnp.where(eye, carry_t, 0.0), axis=0, keepdims=True)
    cum = _cumsum_rows(lf, _tri(SMALL_TILE, "ge")) + carry
    cum_t = _cumsum_lanes(lft, _tri(SMALL_TILE, "le")) + carry_t
    lf_ref[...] = lf[:ls]
    cum_ref[...] = cum[:ls]
    cumt_new_ref[0] = cum_t


def _fx_tile(qm, kb, vb, ck, cq, m, l, acc, mask):
    u = (_dot_nt(qm, kb) + cq) - ck
    if mask is not None:
        u = jnp.where(mask, u, NEG)
    m_new = jnp.maximum(m, jnp.max(u, axis=-1, keepdims=True))
    p = jnp.exp(u - m_new)
    if mask is not None:
        p = jnp.where(mask, p, 0.0)
    alpha = jnp.exp(m - m_new)
    l = alpha * l + jnp.sum(p, axis=-1, keepdims=True)
    acc = alpha * acc + _dot(p.astype(BF16), vb)
    return m_new, l, acc


def _softmax_init(tq, width):
    return (jnp.full((tq, 1), NEG, F32), jnp.zeros((tq, 1), F32), jnp.zeros((tq, width), F32))


def _fox_prompt_kernel(q_ref, k_ref, v_ref, cum_ref, cumt_ref, o_ref, *, seq):
    nfull = seq // KV_TILE
    tail = seq - nfull * KV_TILE
    hp = pl.program_id(1)
    scale = HEAD_DIM ** -0.5

    def kv(ks, n):
        return k_ref[pl.ds(ks, n), :].astype(BF16), v_ref[pl.ds(ks, n), :].astype(BF16)

    def q_block(qs, tq, n_left, diag_n, diag_j):
        q = q_ref[pl.ds(qs, tq), :] * scale
        cum_q = cum_ref[pl.ds(qs, tq), :]
        outs = []
        for hh, qm in enumerate(_head_masks(q)):
            head = 2 * hp + hh
            lane = _iota(cum_q.shape, 1)
            cq = jnp.sum(jnp.where(lane == head, cum_q, 0.0), axis=-1, keepdims=True)
            width = KV_TILE if diag_n == KV_TILE else SMALL_TILE
            kb, vb = kv(qs, diag_n)
            kb, vb = _pad_rows(kb, width), _pad_rows(vb, width)
            ck = cumt_ref[0, diag_j, pl.ds(head, 1), :][:, :width]
            row, col = _iota((tq, width), 0), _iota((tq, width), 1)
            m, l, acc = _softmax_init(tq, LANES)
            m, l, acc = _fx_tile(qm, kb, vb, ck, cq, m, l, acc, col <= row)

            def body(j, carry, qm=qm, cq=cq, head=head):
                kb, vb = kv(pl.multiple_of(j * KV_TILE, KV_TILE), KV_TILE)
                ck = cumt_ref[0, j, pl.ds(head, 1), :]
                return _fx_tile(qm, kb, vb, ck, cq, *carry, None)

            m, l, acc = lax.fori_loop(0, n_left, body, (m, l, acc))
            outs.append(acc / l)
        o_ref[pl.ds(qs, tq), :] = _merge_heads(*outs).astype(o_ref.dtype)

    def full_block(i, carry):
        q_block(pl.multiple_of(i * KV_TILE, KV_TILE), KV_TILE, i, KV_TILE, i)
        return carry

    lax.fori_loop(0, nfull, full_block, 0)
    if tail:
        q_block(nfull * KV_TILE, tail, nfull, tail, nfull)


def _fox_sample_kernel(q_ref, kn_ref, vn_ref, kc_ref, vc_ref, cum_ref, cumt_past_ref, cumt_new_ref,
                       o_ref, *, past):
    nfull = past // KV_TILE
    tail = past - nfull * KV_TILE
    tq = q_ref.shape[0]
    hp = pl.program_id(1)
    q = q_ref[...] * (HEAD_DIM ** -0.5)
    cum_q = cum_ref[...]
    row, col = _iota((tq, SMALL_TILE), 0), _iota((tq, SMALL_TILE), 1)
    outs = []
    for hh, qm in enumerate(_head_masks(q)):
        head = 2 * hp + hh
        lane = _iota(cum_q.shape, 1)
        cq = jnp.sum(jnp.where(lane == head, cum_q, 0.0), axis=-1, keepdims=True)
        m, l, acc = _softmax_init(tq, LANES)
        kb = _pad_rows(kn_ref[...].astype(BF16), SMALL_TILE)
        vb = _pad_rows(vn_ref[...].astype(BF16), SMALL_TILE)
        ck = cumt_new_ref[0, pl.ds(head, 1), :]
        m, l, acc = _fx_tile(qm, kb, vb, ck, cq, m, l, acc, col <= row)
        if tail:
            kb = _pad_rows(kc_ref[0, pl.ds(nfull * KV_TILE, tail), :].astype(BF16), SMALL_TILE)
            vb = _pad_rows(vc_ref[0, pl.ds(nfull * KV_TILE, tail), :].astype(BF16), SMALL_TILE)
            ck = cumt_past_ref[0, nfull, pl.ds(head, 1), :][:, :SMALL_TILE]
            m, l, acc = _fx_tile(qm, kb, vb, ck, cq, m, l, acc, col < tail)

        def body(j, carry, qm=qm, cq=cq, head=head):
            ks = pl.multiple_of(j * KV_TILE, KV_TILE)
            kb = kc_ref[0, pl.ds(ks, KV_TILE), :].astype(BF16)
            vb = vc_ref[0, pl.ds(ks, KV_TILE), :].astype(BF16)
            ck = cumt_past_ref[0, j, pl.ds(head, 1), :]
            return _fx_tile(qm, kb, vb, ck, cq, *carry, None)

        m, l, acc = lax.fori_loop(0, nfull, body, (m, l, acc))
        outs.append(acc / l)
    o_ref[...] = _merge_heads(*outs).astype(o_ref.dtype)


def _fox_attention(hb, qkv, w_f, b_f, cache_k, cache_v, cache_lf, dims):
    bp, lp, bs, ls, past = dims
    hp = N_HEADS // 2
    ntp = pl.cdiv(lp, KV_TILE)
    wf = jnp.pad(w_f, ((0, 0), (0, LANES - N_HEADS))).astype(BF16)
    wft = w_f.T.astype(BF16)
    bfr = jnp.pad(b_f, (0, LANES - N_HEADS)).reshape(1, LANES)
    bfc = b_f.reshape(N_HEADS, 1)
    whole = lambda a: pl.BlockSpec(a.shape, lambda b: (0,) * a.ndim)
    wspecs = [whole(wf), whole(wft), whole(bfr), whole(bfc)]
    lf_p, cum_p, cumt_p = pl.pallas_call(
        functools.partial(_fox_gate_prompt_kernel, seq=lp),
        grid=(bp,),
        in_specs=[pl.BlockSpec((lp, D_MODEL), lambda b: (b, 0))] + wspecs,
        out_specs=[pl.BlockSpec((lp, LANES), lambda b: (b, 0)),
                   pl.BlockSpec((lp, LANES), lambda b: (b, 0)),
                   pl.BlockSpec((1, ntp, N_HEADS, KV_TILE), lambda b: (b, 0, 0, 0))],
        out_shape=[jax.ShapeDtypeStruct((bp * lp, LANES), F32),
                   jax.ShapeDtypeStruct((bp * lp, LANES), F32),
                   jax.ShapeDtypeStruct((bp, ntp, N_HEADS, KV_TILE), F32)],
        compiler_params=_cparams(("arbitrary",), VMEM_LIMIT),
        name="fox_gate_prompt",
    )(hb, wf, wft, bfr, bfc)
    nts = pl.cdiv(past, KV_TILE)
    plft = jnp.pad(jnp.swapaxes(cache_lf, 1, 2), ((0, 0), (0, 0), (0, nts * KV_TILE - past)))
    row0 = bp * lp // ls
    lf_s, cum_s, cumt_past, cumt_new = pl.pallas_call(
        _fox_gate_sample_kernel,
        grid=(bs,),
        in_specs=[pl.BlockSpec((ls, D_MODEL), lambda b: (row0 + b, 0)),
                  pl.BlockSpec((1, N_HEADS, nts * KV_TILE), lambda b: (b, 0, 0))] + wspecs,
        out_specs=[pl.BlockSpec((ls, LANES), lambda b: (b, 0)),
                   pl.BlockSpec((ls, LANES), lambda b: (b, 0)),
                   pl.BlockSpec((1, nts, N_HEADS, KV_TILE), lambda b: (b, 0, 0, 0)),
                   pl.BlockSpec((1, N_HEADS, SMALL_TILE), lambda b: (b, 0, 0))],
        out_shape=[jax.ShapeDtypeStruct((bs * ls, LANES), F32),
                   jax.ShapeDtypeStruct((bs * ls, LANES), F32),
                   jax.ShapeDtypeStruct((bs, nts, N_HEADS, KV_TILE), F32),
                   jax.ShapeDtypeStruct((bs, N_HEADS, SMALL_TILE), F32)],
        compiler_params=_cparams(("arbitrary",), VMEM_LIMIT),
        name="fox_gate_sample",
    )(hb, plft, wf, wft, bfr, bfc)

    o_p = pl.pallas_call(
        functools.partial(_fox_prompt_kernel, seq=lp),
        grid=(bp, hp),
        in_specs=list(_qkv_specs(lp, 0, hp)) + [
            pl.BlockSpec((lp, LANES), lambda b, h: (b, 0)),
            pl.BlockSpec((1, ntp, N_HEADS, KV_TILE), lambda b, h: (b, 0, 0, 0))],
        out_specs=pl.BlockSpec((lp, LANES), lambda b, h: (b, h)),
        out_shape=jax.ShapeDtypeStruct((bp * lp, N_HEADS * HEAD_DIM), BF16),
        compiler_params=_cparams(("arbitrary", "arbitrary"), VMEM_LIMIT),
        name="fox_attn_prompt",
    )(qkv, qkv, qkv, cum_p, cumt_p)
    cache_spec = pl.BlockSpec((1, past, LANES), lambda b, h: (b, 0, h))
    o_s = pl.pallas_call(
        functools.partial(_fox_sample_kernel, past=past),
        grid=(bs, hp),
        in_specs=list(_qkv_specs(ls, row0, hp)) + [
            cache_spec, cache_spec,
            pl.BlockSpec((ls, LANES), lambda b, h: (b, 0)),
            pl.BlockSpec((1, nts, N_HEADS, KV_TILE), lambda b, h: (b, 0, 0, 0)),
            pl.BlockSpec((1, N_HEADS, SMALL_TILE), lambda b, h: (b, 0, 0))],
        out_specs=pl.BlockSpec((ls, LANES), lambda b, h: (b, h)),
        out_shape=jax.ShapeDtypeStruct((bs * ls, N_HEADS * HEAD_DIM), BF16),
        compiler_params=_cparams(("arbitrary", "arbitrary"), VMEM_LIMIT),
        name="fox_attn_sample",
    )(qkv, qkv, qkv, cache_k, cache_v, cum_s, cumt_past, cumt_new)
    o = jnp.concatenate([o_p, o_s], axis=0)
    lf = jnp.concatenate([lf_p, lf_s], axis=0)[:, :N_HEADS]
    return o, lf


def _rms(x, g):
    return x * lax.rsqrt(jnp.mean(x * x, axis=-1, keepdims=True) + RMS_EPS) * g


def _rope_group(x, cos, sin):
    half = QK_ROPE // 2
    lane = _iota(x.shape, 1)
    from_hi = pltpu.roll(x, LANES - half, axis=1)
    from_lo = pltpu.roll(x, half, axis=1)
    rot = jnp.where(lane < half, -from_hi, from_lo)
    return x * cos + rot * sin


def _mla_prep_kernel(c1_ref, gq_ref, gkv_ref, wuq_ref, cos_ref, sin_ref,
                     qn_ref, qr_ref, c_ref, kr_ref):
    nope = N_HEADS * HEAD_DIM
    c1 = c1_ref[...]
    cos, sin = cos_ref[...], sin_ref[...]
    qa = _rms(c1[:, :Q_LORA], gq_ref[...]).astype(BF16)
    q = _dot(qa, wuq_ref[...])
    qn_ref[...] = q[:, :nope].astype(BF16)
    for h in range(N_HEADS):
        lo = nope + h * LANES
        qr_ref[:, h * LANES:(h + 1) * LANES] = _rope_group(q[:, lo:lo + LANES], cos, sin).astype(BF16)
    c_ref[...] = _rms(c1[:, Q_LORA:Q_LORA + KV_LORA], gkv_ref[...])
    kr_ref[...] = _rope_group(c1[:, Q_LORA + KV_LORA:], cos, sin)


def _mla_tile(ql, qr, cb, krb, m, l, acc, mask):
    s = (_dot_nt(ql, cb) + _dot_nt(qr, krb)) * ((HEAD_DIM + QK_ROPE) ** -0.5)
    if mask is not None:
        s = jnp.where(mask, s, NEG)
    m_new = jnp.maximum(m, jnp.max(s, axis=-1, keepdims=True))
    p = jnp.exp(s - m_new)
    if mask is not None:
        p = jnp.where(mask, p, 0.0)
    alpha = jnp.exp(m - m_new)
    l = alpha * l + jnp.sum(p, axis=-1, keepdims=True)
    acc = alpha * acc + _dot(p.astype(BF16), cb)
    return m_new, l, acc


def _mla_heads(qn_ref, qr_ref, wuk_ref, wuv_ref, o_ref, attend):
    for g in range(N_HEADS // 2):
        qn_pair = qn_ref[..., g * LANES:(g + 1) * LANES]
        qn_pair = qn_pair.reshape(qn_pair.shape[-2:])
        out = None
        for hh, qm in enumerate(_head_masks(qn_pair)):
            h = 2 * g + hh
            ql = _dot(qm, wuk_ref[g]).astype(BF16)
            qr = qr_ref[..., h * LANES:(h + 1) * LANES]
            qr = qr.reshape(qr.shape[-2:])
            o_lat = attend(ql, qr)
            part = _dot(o_lat.astype(BF16), wuv_ref[h])
            out = part if out is None else out + part
        o_ref[..., g * LANES:(g + 1) * LANES] = out.reshape(
            o_ref.shape[:-1] + (LANES,)).astype(o_ref.dtype)


def _chunk_of(pos):
    return lax.shift_right_arithmetic(pos - N_META, CHUNK.bit_length() - 1)


def _mla_prompt_kernel(qn_ref, qr_ref, cc_ref, kr_ref, wuk_ref, wuv_ref, o_ref, *, seq):
    i = pl.program_id(1)
    tq = KV_TILE
    qs = i * tq
    diag_ks = jnp.minimum(qs, seq - KV_TILE)
    edge_ks = jnp.minimum(qs + KV_TILE, seq - SMALL_TILE)
    q_chunk = _chunk_of(qs + _iota((tq, 1), 0))

    def chunk_mask(ks, width, first_new):
        kpos = ks + _iota((1, width), 1)
        return (_chunk_of(kpos) <= q_chunk) & (kpos >= first_new)

    def attend(ql, qr):
        def body(j, carry):
            ks = pl.multiple_of(j * KV_TILE, KV_TILE)
            return _mla_tile(ql, qr, cc_ref[0, pl.ds(ks, KV_TILE), :], kr_ref[0, pl.ds(ks, KV_TILE), :],
                             *carry, None)

        carry = lax.fori_loop(0, i, body, _softmax_init(tq, KV_LORA))
        ks = pl.multiple_of(diag_ks, 16)
        carry = _mla_tile(ql, qr, cc_ref[0, pl.ds(ks, KV_TILE), :], kr_ref[0, pl.ds(ks, KV_TILE), :],
                          *carry, chunk_mask(diag_ks, KV_TILE, qs))
        ks = pl.multiple_of(edge_ks, 16)
        m, l, acc = _mla_tile(ql, qr, cc_ref[0, pl.ds(ks, SMALL_TILE), :],
                              kr_ref[0, pl.ds(ks, SMALL_TILE), :],
                              *carry, chunk_mask(edge_ks, SMALL_TILE, qs + KV_TILE))
        return acc / l

    _mla_heads(qn_ref, qr_ref, wuk_ref, wuv_ref, o_ref, attend)


def _mla_sample_kernel(qn_ref, qr_ref, cn_ref, krn_ref, cc_ref, krc_ref, wuk_ref, wuv_ref, o_ref, *, past):
    nfull = past // KV_TILE
    tail = past - nfull * KV_TILE
    tq = qn_ref.shape[0]
    col = _iota((tq, SMALL_TILE), 1)
    n_new = cn_ref.shape[0]

    def attend(ql, qr):
        carry = _softmax_init(tq, KV_LORA)
        cb = _pad_rows(cn_ref[...].astype(BF16), SMALL_TILE)
        krb = _pad_rows(krn_ref[...].astype(BF16), SMALL_TILE)
        carry = _mla_tile(ql, qr, cb, krb, *carry, col < n_new)
        if tail:
            cb = _pad_rows(cc_ref[0, pl.ds(nfull * KV_TILE, tail), :].astype(BF16), SMALL_TILE)
            krb = _pad_rows(krc_ref[0, pl.ds(nfull * KV_TILE, tail), :], SMALL_TILE)
            carry = _mla_tile(ql, qr, cb, krb, *carry, col < tail)

        def body(j, carry):
            ks = pl.multiple_of(j * KV_TILE, KV_TILE)
            return _mla_tile(ql, qr, cc_ref[0, pl.ds(ks, KV_TILE), :].astype(BF16),
                             krc_ref[0, pl.ds(ks, KV_TILE), :], *carry, None)

        m, l, acc = lax.fori_loop(0, nfull, body, carry)
        return acc / l

    _mla_heads(qn_ref, qr_ref, wuk_ref, wuv_ref, o_ref, attend)


def _rope_tables(dims):
    bp, lp, bs, ls, past = dims
    half = QK_ROPE // 2
    pos = jnp.concatenate([jnp.tile(jnp.arange(lp, dtype=jnp.int32), bp),
                           jnp.tile(past + jnp.arange(ls, dtype=jnp.int32), bs)])
    inv = ROPE_THETA ** (-jnp.arange(half, dtype=F32) / half)
    ang = pos.astype(F32)[:, None] * inv
    pad = jnp.zeros((pos.shape[0], LANES - QK_ROPE), F32)
    cos = jnp.concatenate([jnp.cos(ang), jnp.cos(ang), pad], axis=1)
    sin = jnp.concatenate([jnp.sin(ang), jnp.sin(ang), pad], axis=1)
    return cos, sin


def _mla_attention(hb, w, cache_c, cache_kr, dims):
    bp, lp, bs, ls, past = dims
    n = hb.shape[0]
    nope = N_HEADS * HEAD_DIM
    w_dq, g_q, w_uq, w_dkv, g_kv, w_uk, w_uv = w
    w1 = jnp.concatenate([w_dq, w_dkv, jnp.zeros((D_MODEL, LANES - QK_ROPE), F32)], axis=1).astype(BF16)
    c1 = _matmul(hb, w1, name="mla_down_proj")
    wuq = w_uq.reshape(Q_LORA, N_HEADS, HEAD_DIM + QK_ROPE)
    wuq = jnp.concatenate([
        wuq[:, :, :HEAD_DIM].reshape(Q_LORA, nope),
        jnp.pad(wuq[:, :, HEAD_DIM:], ((0, 0), (0, 0), (0, LANES - QK_ROPE))).reshape(Q_LORA, N_HEADS * LANES),
    ], axis=1).astype(BF16)
    cos, sin = _rope_tables(dims)
    tm = 256
    c1w = c1.shape[1]
    row = lambda wd: pl.BlockSpec((tm, wd), lambda i: (i, 0))
    whole = lambda a: pl.BlockSpec(a.shape, lambda i: (0,) * a.ndim)
    gq, gkv = g_q.reshape(1, Q_LORA), g_kv.reshape(1, KV_LORA)
    qn, qr, c, kr = pl.pallas_call(
        _mla_prep_kernel,
        grid=(pl.cdiv(n, tm),),
        in_specs=[row(c1w), whole(gq), whole(gkv), whole(wuq), row(LANES), row(LANES)],
        out_specs=[row(nope), row(N_HEADS * LANES), row(KV_LORA), row(LANES)],
        out_shape=[jax.ShapeDtypeStruct((n, nope), BF16),
                   jax.ShapeDtypeStruct((n, N_HEADS * LANES), BF16),
                   jax.ShapeDtypeStruct((n, KV_LORA), F32),
                   jax.ShapeDtypeStruct((n, LANES), F32)],
        compiler_params=_cparams(("arbitrary",), VMEM_LIMIT),
        name="mla_prep",
    )(c1, gq, gkv, wuq, cos, sin)

    wuk = jnp.transpose(w_uk, (1, 2, 0)).reshape(N_HEADS // 2, LANES, KV_LORA).astype(BF16)
    wuv = jnp.transpose(w_uv, (1, 0, 2))
    zeros = jnp.zeros_like(wuv)
    even = (jnp.arange(N_HEADS) % 2 == 0)[:, None, None]
    wuv = jnp.concatenate([jnp.where(even, wuv, zeros), jnp.where(even, zeros, wuv)], axis=2).astype(BF16)

    np_ = bp * lp
    seq3 = lambda a, wd: a[:np_].reshape(bp, lp, wd)
    ccb = seq3(c, KV_LORA).astype(BF16)
    krb = seq3(kr, LANES).astype(BF16)
    nq = pl.cdiv(lp, KV_TILE)
    wspec = [pl.BlockSpec(wuk.shape, lambda b, i: (0, 0, 0)), pl.BlockSpec(wuv.shape, lambda b, i: (0, 0, 0))]
    o_p = pl.pallas_call(
        functools.partial(_mla_prompt_kernel, seq=lp),
        grid=(bp, nq),
        in_specs=[pl.BlockSpec((1, KV_TILE, nope), lambda b, i: (b, i, 0)),
                  pl.BlockSpec((1, KV_TILE, N_HEADS * LANES), lambda b, i: (b, i, 0)),
                  pl.BlockSpec((1, lp, KV_LORA), lambda b, i: (b, 0, 0)),
                  pl.BlockSpec((1, lp, LANES), lambda b, i: (b, 0, 0))] + wspec,
        out_specs=pl.BlockSpec((1, KV_TILE, nope), lambda b, i: (b, i, 0)),
        out_shape=jax.ShapeDtypeStruct((bp, lp, nope), BF16),
        compiler_params=_cparams(("arbitrary", "arbitrary"), VMEM_LIMIT),
        name="mla_attn_prompt",
    )(seq3(qn, nope), seq3(qr, N_HEADS * LANES), ccb, krb, wuk, wuv)

    row0 = np_ // ls
    krc = jnp.pad(cache_kr, ((0, 0), (0, 0), (0, LANES - QK_ROPE))).astype(BF16)
    wspec1 = [pl.BlockSpec(wuk.shape, lambda b: (0, 0, 0)), pl.BlockSpec(wuv.shape, lambda b: (0, 0, 0))]
    o_s = pl.pallas_call(
        functools.partial(_mla_sample_kernel, past=past),
        grid=(bs,),
        in_specs=[pl.BlockSpec((ls, nope), lambda b: (row0 + b, 0)),
                  pl.BlockSpec((ls, N_HEADS * LANES), lambda b: (row0 + b, 0)),
                  pl.BlockSpec((ls, KV_LORA), lambda b: (row0 + b, 0)),
                  pl.BlockSpec((ls, LANES), lambda b: (row0 + b, 0)),
                  pl.BlockSpec((1, past, KV_LORA), lambda b: (b, 0, 0)),
                  pl.BlockSpec((1, past, LANES), lambda b: (b, 0, 0))] + wspec1,
        out_specs=pl.BlockSpec((ls, nope), lambda b: (b, 0)),
        out_shape=jax.ShapeDtypeStruct((bs * ls, nope), BF16),
        compiler_params=_cparams(("arbitrary",), VMEM_LIMIT),
        name="mla_attn_sample",
    )(qn, qr, c, kr, cache_c, krc, wuk, wuv)
    o = jnp.concatenate([o_p.reshape(np_, nope), o_s], axis=0)
    return o, c, kr[:, :QK_ROPE]


def _router_kernel(x_ref, w_ref, b_ref, idx_ref, gate_ref):
    xh, xm, _ = _split3(x_ref[...])
    wh, wm, _ = _split3(w_ref[...])
    lg = _dot(xh, wh) + (_dot(xh, wm) + _dot(xm, wh)) + b_ref[...]
    lane = _iota(lg.shape, 1)
    lane_f = lane.astype(F32)
    lg = jnp.where(lane < N_EXPERTS, lg, -jnp.inf)
    vals, idxs = [], []
    for _ in range(TOP_K):
        m = jnp.max(lg, axis=-1, keepdims=True)
        sel = jnp.min(jnp.where(lg == m, lane_f, float(LANES)), axis=-1, keepdims=True)
        vals.append(m)
        idxs.append(sel.astype(jnp.int32))
        lg = jnp.where(lane_f == sel, -jnp.inf, lg)
    es = [jnp.exp(v - vals[0]) for v in vals]
    tot = es[0] + es[1] + es[2] + es[3]
    idx = jnp.zeros(lg.shape, jnp.int32)
    gate = jnp.zeros(lg.shape, F32)
    for k in range(TOP_K):
        idx = jnp.where(lane == k, idxs[k], idx)
        gate = jnp.where(lane == k, es[k] / tot, gate)
    idx_ref[...] = idx
    gate_ref[...] = gate


def _router(h, w_r, b_r, *, tm=512):
    M, D = h.shape
    w = jnp.pad(w_r, ((0, 0), (0, LANES - N_EXPERTS)))
    b = jnp.pad(b_r, (0, LANES - N_EXPERTS)).reshape(1, LANES)
    row = pl.BlockSpec((tm, LANES), lambda i: (i, 0))
    idx, gate = pl.pallas_call(
        _router_kernel,
        grid=(pl.cdiv(M, tm),),
        in_specs=[pl.BlockSpec((tm, D), lambda i: (i, 0)),
                  pl.BlockSpec((D, LANES), lambda i: (0, 0)),
                  pl.BlockSpec((1, LANES), lambda i: (0, 0))],
        out_specs=[row, row],
        out_shape=[jax.ShapeDtypeStruct((M, LANES), jnp.int32), jax.ShapeDtypeStruct((M, LANES), F32)],
        compiler_params=_cparams(("arbitrary",)),
        name="moe_router",
    )(h, w, b)
    return idx[:, :TOP_K], gate[:, :TOP_K]


def _expert_kernel(te_ref, nt_ref, x_ref, g_ref, wgu_ref, bgu_ref, wdn_ref, bdn_ref, o_ref,
                   wgu_bf, wdn_bf):
    t = pl.program_id(0)

    @pl.when(t < nt_ref[0])
    def _():
        prev = te_ref[jnp.maximum(t - 1, 0)]

        @pl.when((t == 0) | (te_ref[t] != prev))
        def _():
            wgu_bf[...] = wgu_ref[0].astype(BF16)
            wdn_bf[...] = wdn_ref[0].astype(BF16)

        gu = _dot(x_ref[...], wgu_bf[...]) + bgu_ref[0]
        gate = jnp.minimum(gu[:, :D_EXPERT], SWIGLU_LIMIT)
        up = jnp.clip(gu[:, D_EXPERT:], -SWIGLU_LIMIT, SWIGLU_LIMIT)
        act = (up + 1.0) * gate * jax.nn.sigmoid(gate * SWIGLU_ALPHA)
        y = _dot(act.astype(BF16), wdn_bf[...]) + bdn_ref[0]
        o_ref[...] = g_ref[...] * y


def _moe(h, hb, w_r, b_r, w_gu, b_gu, w_dn, b_dn, *, tm=256):
    n, d = h.shape
    top_i, gates = _router(h, w_r, b_r)
    e_flat = top_i.reshape(-1)
    onehot = (e_flat[:, None] == jnp.arange(N_EXPERTS, dtype=jnp.int32)[None, :]).astype(jnp.int32)
    csum = jnp.cumsum(onehot, axis=0)
    rank = jnp.take_along_axis(csum, e_flat[:, None], axis=1)[:, 0] - 1
    counts = csum[-1]
    padded = ((counts + tm - 1) // tm) * tm
    ends = jnp.cumsum(padded)
    dest = (ends - padded)[e_flat] + rank
    n_tiles = (n * TOP_K + N_EXPERTS * (tm - 1)) // tm + 1
    slots = n_tiles * tm
    used = (ends[-1] // tm).astype(jnp.int32)
    tile_e = jnp.searchsorted(ends, jnp.arange(n_tiles, dtype=jnp.int32) * tm, side="right")
    last_e = jnp.searchsorted(ends, (used - 1) * tm, side="right")
    tile_e = jnp.where(jnp.arange(n_tiles) < used, tile_e, last_e).astype(jnp.int32)
    token = jnp.zeros((slots,), jnp.int32).at[dest].set(jnp.arange(n * TOP_K, dtype=jnp.int32) // TOP_K)
    gate_s = jnp.zeros((slots,), F32).at[dest].set(gates.reshape(-1))
    xs = jnp.take(hb, token, axis=0)

    tile = lambda t, te, nt: (jnp.minimum(t, nt[0] - 1), 0)
    expert = lambda t, te, nt: (te[t], 0, 0)
    y = pl.pallas_call(
        _expert_kernel,
        grid_spec=pltpu.PrefetchScalarGridSpec(
            num_scalar_prefetch=2,
            grid=(n_tiles,),
            in_specs=[pl.BlockSpec((tm, d), tile),
                      pl.BlockSpec((tm, 1), tile),
                      pl.BlockSpec((1, d, 2 * D_EXPERT), expert),
                      pl.BlockSpec((1, 1, 2 * D_EXPERT), expert),
                      pl.BlockSpec((1, D_EXPERT, d), expert),
                      pl.BlockSpec((1, 1, d), expert)],
            out_specs=pl.BlockSpec((tm, d), tile),
            scratch_shapes=[pltpu.VMEM((d, 2 * D_EXPERT), BF16), pltpu.VMEM((D_EXPERT, d), BF16)]),
        out_shape=jax.ShapeDtypeStruct((slots, d), F32),
        compiler_params=_cparams(("arbitrary",), VMEM_LIMIT),
        name="moe_experts",
    )(tile_e, used.reshape(1), xs, gate_s.reshape(slots, 1), w_gu, b_gu.reshape(N_EXPERTS, 1, -1),
      w_dn, b_dn.reshape(N_EXPERTS, 1, -1))
    return jnp.take(y, dest, axis=0).reshape(n, TOP_K, d).sum(axis=1)


def kernel(x_prompt, x_sample, cache_sb_k, cache_sb_v, cache_fox_k, cache_fox_v, cache_fox_logf,
           cache_mla_ckv, cache_mla_krope, meta_tokens, a_w_qkv, a_w_o, b_w_qkv, b_w_f, b_b_f, b_w_o,
           c_w_dq, c_g_q, c_w_uq, c_w_dkv, c_g_kv, c_w_uk, c_w_uv, c_w_o, ln1_g, ln1_b, ln2_g, ln2_b,
           moe_w_router, moe_b_router, moe_w_gate_up, moe_b_gate_up, moe_w_down, moe_b_down):
    bp, sp, d = x_prompt.shape
    bs, ls, _ = x_sample.shape
    lp = N_META + sp
    past = cache_sb_k.shape[2]
    dims = (bp, lp, bs, ls, past)
    np_ = bp * lp
    hd = N_HEADS * HEAD_DIM

    meta = jnp.broadcast_to(meta_tokens[None], (bp, N_META, d))
    h = jnp.concatenate([jnp.concatenate([meta, x_prompt], axis=1).reshape(np_, d),
                         x_sample.reshape(bs * ls, d)], axis=0)
    hb = h.astype(BF16)
    new = {k: [] for k in ("a_k", "a_v", "b_k", "b_v", "b_lf", "c_c", "c_kr")}

    def heads(x, lo):
        return x[:, lo:lo + hd].reshape(-1, N_HEADS, HEAD_DIM)

    for i in range(DEPTH):
        kind, j = i % 3, i // 3
        if kind == 0:
            qkv = _matmul(hb, a_w_qkv[j].astype(BF16), name="sb_qkv_proj", tn=hd)
            o = _sb_attention(qkv, cache_sb_k[j].reshape(bs, past, hd), cache_sb_v[j].reshape(bs, past, hd), dims)
            y = _matmul(o, a_w_o[j].astype(BF16), name="sb_out_proj")
            new["a_k"].append(heads(qkv, hd))
            new["a_v"].append(heads(qkv, 2 * hd))
        elif kind == 1:
            qkv = _matmul(hb, b_w_qkv[j].astype(BF16), name="fox_qkv_proj", tn=hd)
            o, lf = _fox_attention(hb, qkv, b_w_f[j], b_b_f[j], cache_fox_k[j].reshape(bs, past, hd),
                                   cache_fox_v[j].reshape(bs, past, hd), cache_fox_logf[j], dims)
            y = _matmul(o, b_w_o[j].astype(BF16), name="fox_out_proj")
            new["b_k"].append(heads(qkv, hd))
            new["b_v"].append(heads(qkv, 2 * hd))
            new["b_lf"].append(lf)
        else:
            o, c, kr = _mla_attention(hb, (c_w_dq[j], c_g_q[j], c_w_uq[j], c_w_dkv[j], c_g_kv[j],
                                           c_w_uk[j], c_w_uv[j]), cache_mla_ckv[j], cache_mla_krope[j], dims)
            y = _matmul(o, c_w_o[j].astype(BF16), name="mla_out_proj")
            new["c_c"].append(c)
            new["c_kr"].append(kr)
        h, hb = _deepnorm_ln(h, y, ln1_g[i], ln1_b[i], name="ln_mixer")
        f = _moe(h, hb, moe_w_router[i], moe_b_router[i], moe_w_gate_up[i], moe_b_gate_up[i],
                 moe_w_down[i], moe_b_down[i])
        h, hb = _deepnorm_ln(h, f, ln2_g[i], ln2_b[i], name="ln_moe")

    def split(rows, tail_shape):
        xs = jnp.stack(rows)
        return (xs[:, :np_].reshape((len(rows), bp, lp) + tail_shape),
                xs[:, np_:].reshape((len(rows), bs, ls) + tail_shape))

    outs_p, outs_s = [], []
    for key, tail_shape in (("a_k", (N_HEADS, HEAD_DIM)), ("a_v", (N_HEADS, HEAD_DIM)),
                            ("b_k", (N_HEADS, HEAD_DIM)), ("b_v", (N_HEADS, HEAD_DIM)),
                            ("b_lf", (N_HEADS,)), ("c_c", (KV_LORA,)), ("c_kr", (QK_ROPE,))):
        p, s = split(new[key], tail_shape)
        outs_p.append(p)
        outs_s.append(s)
    y_prompt = h[:np_].reshape(bp, lp, d)[:, N_META:]
    y_sample = h[np_:].reshape(bs, ls, d)
    return (y_prompt, y_sample) + tuple(outs_p) + tuple(outs_s)
```

```python
import functools

import jax
import jax.numpy as jnp
from jax import lax
from jax.experimental import pallas as pl
from jax.experimental.pallas import tpu as pltpu

F32 = jnp.float32
BF16 = jnp.bfloat16

D_MODEL = 1024
N_META = 16
CHUNK = 64
N_HEADS = 16
HEAD_DIM = 64
Q_LORA = 384
KV_LORA = 256
QK_ROPE = 32
ROPE_THETA = 10000.0
N_EXPERTS = 32
TOP_K = 4
D_EXPERT = 1024
SWIGLU_LIMIT = 7.0
SWIGLU_ALPHA = 1.702
DEPTH = 4
DN_ALPHA = (2 * DEPTH) ** 0.25
LN_EPS = 1e-5
RMS_EPS = 1e-6

LANES = 128
KV_TILE = 256
SMALL_TILE = 128
Q_BLOCK = 2 * KV_TILE
WIDE = 4
MLA_HEAD_GROUP = 4
MOE_TILE = 256
NEG = -1e30
VMEM_LIMIT = 56 * 2 ** 20


def _cparams(sem, vmem=None):
    return pltpu.CompilerParams(dimension_semantics=sem, vmem_limit_bytes=vmem)


def _row_tile(n, cap, mult=16):
    t = cap - cap % mult
    while t > mult and n % t:
        t -= mult
    assert n % t == 0, (n, cap)
    return t


def _dot(a, b):
    return jnp.dot(a, b, preferred_element_type=F32)


def _dot_nt(a, b):
    return lax.dot_general(a, b, (((1,), (1,)), ((), ())), preferred_element_type=F32)


def _split3(x):
    hi = x.astype(BF16)
    r = x - hi.astype(F32)
    mid = r.astype(BF16)
    lo = (r - mid.astype(F32)).astype(BF16)
    return hi, mid, lo


def _iota(shape, axis):
    return lax.broadcasted_iota(jnp.int32, shape, axis)


def _tri(n, op):
    r, c = _iota((n, n), 0), _iota((n, n), 1)
    m = {"gt": r > c, "ge": r >= c, "le": r <= c, "lt": r < c}[op]
    return jnp.where(m, 1.0, 0.0).astype(BF16)


def _pad_rows(x, n):
    if x.shape[0] == n:
        return x
    return jnp.concatenate([x, jnp.zeros((n - x.shape[0],) + x.shape[1:], x.dtype)], axis=0)


def _pad_lanes(x, n):
    if x.shape[1] == n:
        return x
    return jnp.concatenate([x, jnp.zeros((x.shape[0], n - x.shape[1]), x.dtype)], axis=1)


def _lane_pick(x, k):
    return jnp.sum(jnp.where(_iota(x.shape, 1) == k, x, 0), axis=-1, keepdims=True)


def _mm_kernel(x_ref, w_ref, o_ref):
    o_ref[...] = _dot(x_ref[...], w_ref[...]).astype(o_ref.dtype)


def _matmul(x, w, *, name, tn=None, out_dtype=F32):
    M, K = x.shape
    N = w.shape[1]
    tn = N if tn is None else tn
    tm = _row_tile(M, 512)
    return pl.pallas_call(
        _mm_kernel,
        grid=(N // tn, M // tm),
        in_specs=[pl.BlockSpec((tm, K), lambda j, i: (i, 0)),
                  pl.BlockSpec((K, tn), lambda j, i: (0, j))],
        out_specs=pl.BlockSpec((tm, tn), lambda j, i: (i, j)),
        out_shape=jax.ShapeDtypeStruct((M, N), out_dtype),
        compiler_params=_cparams(("arbitrary", "arbitrary")),
        name=name,
    )(x, w)


def _layer_norm(x, g, b):
    mu = jnp.mean(x, axis=-1, keepdims=True)
    xc = x - mu
    var = jnp.mean(xc * xc, axis=-1, keepdims=True)
    return xc * lax.rsqrt(var + LN_EPS) * g + b


def _ln_kernel(h_ref, y_ref, g_ref, b_ref, o_ref, ob_ref):
    o = _layer_norm(DN_ALPHA * h_ref[...] + y_ref[...], g_ref[...], b_ref[...])
    o_ref[...] = o
    ob_ref[...] = o.astype(BF16)


def _ln_combine_kernel(h_ref, y0_ref, y1_ref, y2_ref, y3_ref, gate_ref, g_ref, b_ref, o_ref, ob_ref):
    gate = gate_ref[...]
    f = None
    for k, y_ref in enumerate((y0_ref, y1_ref, y2_ref, y3_ref)):
        term = _lane_pick(gate, k) * y_ref[...]
        f = term if f is None else f + term
    o = _layer_norm(DN_ALPHA * h_ref[...] + f, g_ref[...], b_ref[...])
    o_ref[...] = o
    ob_ref[...] = o.astype(BF16)


def _deepnorm_ln(h, ys, gates, g, b, *, name):
    M, D = h.shape
    tm = _row_tile(M, 512)
    row = pl.BlockSpec((tm, D), lambda i: (i, 0))
    vec = pl.BlockSpec((1, D), lambda i: (0, 0))
    extra = [] if gates is None else [pl.BlockSpec((tm, LANES), lambda i: (i, 0))]
    args = list(ys) + ([] if gates is None else [gates])
    return pl.pallas_call(
        _ln_kernel if gates is None else _ln_combine_kernel,
        grid=(M // tm,),
        in_specs=[row] + [row] * len(ys) + extra + [vec, vec],
        out_specs=[row, row],
        out_shape=[jax.ShapeDtypeStruct((M, D), F32), jax.ShapeDtypeStruct((M, D), BF16)],
        compiler_params=_cparams(("arbitrary",)),
        name=name,
    )(h, *args, g.reshape(1, D), b.reshape(1, D))


def _stack_heads(q):
    lane = _iota(q.shape, 1)
    q0 = jnp.where(lane < HEAD_DIM, q, 0.0).astype(BF16)
    q1 = jnp.where(lane >= HEAD_DIM, q, 0.0).astype(BF16)
    return jnp.concatenate([q0, q1], axis=0)


def _unstack_heads(acc, tq):
    lane = _iota((tq, LANES), 1)
    return jnp.where(lane < HEAD_DIM, acc[:tq], acc[tq:])


def _stacked_rows(n_stack, tq, width):
    row = _iota((n_stack * tq, width), 0)
    for _ in range(n_stack - 1):
        row = jnp.where(row >= tq, row - tq, row)
    return row


def _softmax_init(rows, width):
    return (jnp.full((rows, 1), NEG, F32), jnp.zeros((rows, 1), F32), jnp.zeros((rows, width), F32))


def _softmax_step(u, vb, m, l, acc, mask):
    if mask is not None:
        u = jnp.where(mask, u, NEG)
    m_new = jnp.maximum(m, jnp.max(u, axis=-1, keepdims=True))
    p = jnp.exp(u - m_new)
    if mask is not None:
        p = jnp.where(mask, p, 0.0)
    alpha = jnp.exp(m - m_new)
    l = alpha * l + jnp.sum(p, axis=-1, keepdims=True)
    acc = alpha * acc + _dot(p.astype(BF16), vb)
    return m_new, l, acc


def _qkv_specs(seq, row_block0, n_col_blocks):
    def spec(part):
        return pl.BlockSpec((seq, LANES), lambda b, hp: (row_block0 + b, part * n_col_blocks + hp))
    return spec(0), spec(1), spec(2)


def _prompt_blocks(seq, q_block):
    nfull, tail = divmod(seq, KV_TILE)
    assert nfull % 2 == 0 and nfull >= 2, seq
    nblk = nfull // 2

    def body(i, carry):
        qs = pl.multiple_of(i * Q_BLOCK, Q_BLOCK)
        q_block(qs, Q_BLOCK, [(qs + KV_TILE, KV_TILE, KV_TILE), (qs, KV_TILE, KV_TILE)], 2 * i)
        return carry

    lax.fori_loop(0, nblk - 1, body, 0)
    qs = (nblk - 1) * Q_BLOCK
    diag = [(qs + KV_TILE, KV_TILE, KV_TILE), (qs, KV_TILE, KV_TILE)]
    if tail:
        diag = [(nfull * KV_TILE, tail, SMALL_TILE)] + diag
    q_block(qs, Q_BLOCK + tail, diag, 2 * (nblk - 1))


def _load_tile(ref, ks, n, width, lead=()):
    if not isinstance(ks, int):
        ks = pl.multiple_of(ks, 16)
    return _pad_rows(ref[lead + (pl.ds(ks, n), slice(None))].astype(BF16), width)


def _sb_tile(qm, kb, vb, r, acc, mask, tri_gt):
    s = _dot_nt(qm, kb)
    ls = jnp.minimum(s, 0.0) - jnp.log(1.0 + jnp.exp(-jnp.abs(s)))
    l = ls - s
    if mask is not None:
        l = jnp.where(mask, l, 0.0)
    w = tri_gt.shape[0]
    parts = []
    for lo in reversed(range(0, s.shape[1], w)):
        lb = l[:, lo:lo + w]
        lhi = lb.astype(BF16)
        llo = (lb - lhi.astype(F32)).astype(BF16)
        c = _dot(lhi, tri_gt) + _dot(llo, tri_gt)
        parts.append(jnp.exp(ls[:, lo:lo + w] + c + r))
        r = r + jnp.sum(lb, axis=-1, keepdims=True)
    a = parts[0] if len(parts) == 1 else jnp.concatenate(parts[::-1], axis=1)
    if mask is not None:
        a = jnp.where(mask, a, 0.0)
    acc = acc + _dot(a.astype(BF16), vb)
    return r, acc


def _sb_prompt_kernel(q_ref, k_ref, v_ref, o_ref, *, seq):
    tris = {KV_TILE: _tri(KV_TILE, "gt"), SMALL_TILE: _tri(SMALL_TILE, "gt")}
    scale = HEAD_DIM ** -0.5

    def q_block(qs, tq, diag, n_left):
        qm = _stack_heads(q_ref[pl.ds(qs, tq), :] * scale)
        r = jnp.zeros((2 * tq, 1), F32)
        acc = jnp.zeros((2 * tq, LANES), F32)
        for ks, n, width in diag:
            kb, vb = _load_tile(k_ref, ks, n, width), _load_tile(v_ref, ks, n, width)
            mask = ks + _iota((2 * tq, width), 1) < qs + _stacked_rows(2, tq, width)
            r, acc = _sb_tile(qm, kb, vb, r, acc, mask, tris[width])

        def body(jj, carry):
            ks = (n_left - 1 - jj) * KV_TILE
            kb, vb = _load_tile(k_ref, ks, KV_TILE, KV_TILE), _load_tile(v_ref, ks, KV_TILE, KV_TILE)
            return _sb_tile(qm, kb, vb, carry[0], carry[1], None, tris[KV_TILE])

        r, acc = lax.fori_loop(0, n_left, body, (r, acc))
        o_ref[pl.ds(qs, tq), :] = _unstack_heads(acc, tq).astype(o_ref.dtype)

    _prompt_blocks(seq, q_block)


def _sb_sample_kernel(q_ref, kn_ref, vn_ref, kc_ref, vc_ref, o_ref, *, past):
    nfull, tail = divmod(past, KV_TILE)
    tq = q_ref.shape[0]
    tri_full, tri_small = _tri(KV_TILE, "gt"), _tri(SMALL_TILE, "gt")
    qm = _stack_heads(q_ref[...] * (HEAD_DIM ** -0.5))
    row, col = _stacked_rows(2, tq, SMALL_TILE), _iota((2 * tq, SMALL_TILE), 1)
    r = jnp.zeros((2 * tq, 1), F32)
    acc = jnp.zeros((2 * tq, LANES), F32)
    n_new = kn_ref.shape[0]
    r, acc = _sb_tile(qm, _load_tile(kn_ref, 0, n_new, SMALL_TILE), _load_tile(vn_ref, 0, n_new, SMALL_TILE),
                      r, acc, col < row, tri_small)
    if tail:
        r, acc = _sb_tile(qm, _load_tile(kc_ref, nfull * KV_TILE, tail, SMALL_TILE, (0,)),
                          _load_tile(vc_ref, nfull * KV_TILE, tail, SMALL_TILE, (0,)),
                          r, acc, col < tail, tri_small)

    def cache_tile(ks, n, carry):
        return _sb_tile(qm, _load_tile(kc_ref, ks, n, n, (0,)), _load_tile(vc_ref, ks, n, n, (0,)),
                        carry[0], carry[1], None, tri_full)

    n_wide = nfull // WIDE
    carry = (r, acc)
    for j in reversed(range(n_wide * WIDE, nfull)):
        carry = cache_tile(j * KV_TILE, KV_TILE, carry)
    r, acc = lax.fori_loop(0, n_wide, lambda jj, c: cache_tile((n_wide - 1 - jj) * WIDE * KV_TILE,
                                                                 WIDE * KV_TILE, c), carry)
    o_ref[...] = _unstack_heads(acc, tq).astype(o_ref.dtype)


def _sb_attention(qkv, cache_k, cache_v, dims):
    bp, lp, bs, ls, past = dims
    hp = N_HEADS // 2
    o_p = pl.pallas_call(
        functools.partial(_sb_prompt_kernel, seq=lp),
        grid=(bp, hp),
        in_specs=list(_qkv_specs(lp, 0, hp)),
        out_specs=pl.BlockSpec((lp, LANES), lambda b, h: (b, h)),
        out_shape=jax.ShapeDtypeStruct((bp * lp, N_HEADS * HEAD_DIM), BF16),
        compiler_params=_cparams(("arbitrary", "arbitrary"), VMEM_LIMIT),
        name="sb_attn_prompt",
    )(qkv, qkv, qkv)
    cache_spec = pl.BlockSpec((1, past, LANES), lambda b, h: (b, 0, h))
    o_s = pl.pallas_call(
        functools.partial(_sb_sample_kernel, past=past),
        grid=(bs, hp),
        in_specs=list(_qkv_specs(ls, bp * lp // ls, hp)) + [cache_spec, cache_spec],
        out_specs=pl.BlockSpec((ls, LANES), lambda b, h: (b, h)),
        out_shape=jax.ShapeDtypeStruct((bs * ls, N_HEADS * HEAD_DIM), BF16),
        compiler_params=_cparams(("arbitrary", "arbitrary"), VMEM_LIMIT),
        name="sb_attn_sample",
    )(qkv, qkv, qkv, cache_k, cache_v)
    return jnp.concatenate([o_p, o_s], axis=0)


def _log_sigmoid(x):
    return jnp.minimum(x, 0.0) - jnp.log1p(jnp.exp(-jnp.abs(x)))


def _cumsum_rows(x, tri_ge):
    hi, mid, lo = _split3(x)
    return _dot(tri_ge, hi) + _dot(tri_ge, mid) + _dot(tri_ge, lo)


def _cumsum_lanes(x, tri_le):
    hi, mid, lo = _split3(x)
    return _dot(hi, tri_le) + _dot(mid, tri_le) + _dot(lo, tri_le)


def _gate_chunk(hc, valid, wf_ref, wft_ref, bf_row_ref, bf_col_ref):
    n = hc.shape[0]
    g = _dot(hc, wf_ref[...]) + bf_row_ref[...]
    gt = _dot_nt(wft_ref[...], hc) + bf_col_ref[...]
    lf, lft = _log_sigmoid(g), _log_sigmoid(gt)
    if valid < n:
        lf = jnp.where(_iota(lf.shape, 0) < valid, lf, 0.0)
        lft = jnp.where(_iota(lft.shape, 1) < valid, lft, 0.0)
    return lf, lft


def _fox_gate_prompt_kernel(h_ref, wf_ref, wft_ref, bfr_ref, bfc_ref, lf_ref, cum_ref, cumt_ref, *, seq):
    nfull, tail = divmod(seq, KV_TILE)
    carry = jnp.zeros((1, LANES), F32)
    carry_t = jnp.zeros((N_HEADS, 1), F32)
    for j in range(nfull + (1 if tail else 0)):
        n = KV_TILE if j < nfull else SMALL_TILE
        valid = KV_TILE if j < nfull else tail
        hc = _pad_rows(h_ref[j * KV_TILE:j * KV_TILE + valid, :], n)
        lf, lft = _gate_chunk(hc, valid, wf_ref, wft_ref, bfr_ref, bfc_ref)
        cum = _cumsum_rows(lf, _tri(n, "ge")) + carry
        cum_t = _cumsum_lanes(lft, _tri(n, "le")) + carry_t
        lf_ref[j * KV_TILE:j * KV_TILE + valid, :] = lf[:valid]
        cum_ref[j * KV_TILE:j * KV_TILE + valid, :] = cum[:valid]
        cumt_ref[0, j] = _pad_lanes(cum_t, KV_TILE)
        carry = cum[valid - 1:valid, :]
        carry_t = cum_t[:, valid - 1:valid]


def _fox_gate_sample_kernel(h_ref, plft_ref, wf_ref, wft_ref, bfr_ref, bfc_ref,
                            lf_ref, cum_ref, cumt_past_ref, cumt_new_ref):
    ntiles = plft_ref.shape[2] // KV_TILE
    carry_t = jnp.zeros((N_HEADS, 1), F32)
    tri_le = _tri(KV_TILE, "le")
    for j in range(ntiles):
        cum_t = _cumsum_lanes(plft_ref[0, :, j * KV_TILE:(j + 1) * KV_TILE], tri_le) + carry_t
        cumt_past_ref[0, j] = cum_t
        carry_t = cum_t[:, KV_TILE - 1:KV_TILE]
    ls = h_ref.shape[0]
    hc = _pad_rows(h_ref[...], SMALL_TILE)
    lf, lft = _gate_chunk(hc, ls, wf_ref, wft_ref, bfr_ref, bfc_ref)
    eye = _iota((N_HEADS, LANES), 0) == _iota((N_HEADS, LANES), 1)
    carry = jnp.sum(jnp.where(eye, carry_t, 0.0), axis=0, keepdims=True)
    cum = _cumsum_rows(lf, _tri(SMALL_TILE, "ge")) + carry
    cum_t = _cumsum_lanes(lft, _tri(SMALL_TILE, "le")) + carry_t
    lf_ref[...] = lf[:ls]
    cum_ref[...] = cum[:ls]
    cumt_new_ref[0] = cum_t


def _fx_tile(qm, kb, vb, ck, cq, m, l, acc, mask):
    s = _dot_nt(qm, kb)
    tq = s.shape[0] // 2
    u = jnp.concatenate([(s[:tq] + cq[:tq]) - ck[0], (s[tq:] + cq[tq:]) - ck[1]], axis=0)
    return _softmax_step(u, vb, m, l, acc, mask)


def _head_rows(ref, lead, head0, width):
    return tuple(ref[lead + (pl.ds(head0 + hh, 1), slice(None))][:, :width] for hh in range(2))


def _fox_prompt_kernel(q_ref, k_ref, v_ref, cum_ref, cumt_ref, o_ref, *, seq):
    head0 = 2 * pl.program_id(1)
    scale = HEAD_DIM ** -0.5

    def ck_rows(j, width):
        return _head_rows(cumt_ref, (0, j), head0, width)

    def q_block(qs, tq, diag, n_left):
        qm = _stack_heads(q_ref[pl.ds(qs, tq), :] * scale)
        cum_q = cum_ref[pl.ds(qs, tq), :]
        cq = jnp.concatenate([_lane_pick(cum_q, head0), _lane_pick(cum_q, head0 + 1)], axis=0)
        carry = _softmax_init(2 * tq, LANES)
        for ks, n, width in diag:
            kb, vb = _load_tile(k_ref, ks, n, width), _load_tile(v_ref, ks, n, width)
            j = ks // KV_TILE
            mask = ks + _iota((2 * tq, width), 1) <= qs + _stacked_rows(2, tq, width)
            carry = _fx_tile(qm, kb, vb, ck_rows(j, width), cq, *carry, mask)

        def body(j, carry):
            ks = j * KV_TILE
            kb, vb = _load_tile(k_ref, ks, KV_TILE, KV_TILE), _load_tile(v_ref, ks, KV_TILE, KV_TILE)
            return _fx_tile(qm, kb, vb, ck_rows(j, KV_TILE), cq, *carry, None)

        m, l, acc = lax.fori_loop(0, n_left, body, carry)
        o_ref[pl.ds(qs, tq), :] = _unstack_heads(acc / l, tq).astype(o_ref.dtype)

    _prompt_blocks(seq, q_block)


def _fox_sample_kernel(q_ref, kn_ref, vn_ref, kc_ref, vc_ref, cum_ref, cumt_past_ref, cumt_new_ref,
                       o_ref, *, past):
    nfull, tail = divmod(past, KV_TILE)
    tq = q_ref.shape[0]
    head0 = 2 * pl.program_id(1)
    qm = _stack_heads(q_ref[...] * (HEAD_DIM ** -0.5))
    cum_q = cum_ref[...]
    cq = jnp.concatenate([_lane_pick(cum_q, head0), _lane_pick(cum_q, head0 + 1)], axis=0)
    row, col = _stacked_rows(2, tq, SMALL_TILE), _iota((2 * tq, SMALL_TILE), 1)
    n_new = kn_ref.shape[0]
    carry = _softmax_init(2 * tq, LANES)
    carry = _fx_tile(qm, _load_tile(kn_ref, 0, n_new, SMALL_TILE), _load_tile(vn_ref, 0, n_new, SMALL_TILE),
                     _head_rows(cumt_new_ref, (0,), head0, SMALL_TILE), cq, *carry, col <= row)
    if tail:
        carry = _fx_tile(qm, _load_tile(kc_ref, nfull * KV_TILE, tail, SMALL_TILE, (0,)),
                         _load_tile(vc_ref, nfull * KV_TILE, tail, SMALL_TILE, (0,)),
                         _head_rows(cumt_past_ref, (0, nfull), head0, SMALL_TILE), cq, *carry, col < tail)

    def cache_tile(j, n_tiles, carry):
        ks, n = j * KV_TILE, n_tiles * KV_TILE
        rows = [_head_rows(cumt_past_ref, (0, j + t), head0, KV_TILE) for t in range(n_tiles)]
        ck = tuple(rows[0][hh] if n_tiles == 1 else jnp.concatenate([r[hh] for r in rows], axis=1)
                   for hh in range(2))
        return _fx_tile(qm, _load_tile(kc_ref, ks, n, n, (0,)), _load_tile(vc_ref, ks, n, n, (0,)),
                        ck, cq, *carry, None)

    n_wide = nfull // WIDE
    for j in range(n_wide * WIDE, nfull):
        carry = cache_tile(j, 1, carry)
    m, l, acc = lax.fori_loop(0, n_wide, lambda jw, c: cache_tile(jw * WIDE, WIDE, c), carry)
    o_ref[...] = _unstack_heads(acc / l, tq).astype(o_ref.dtype)


def _fox_attention(hb, qkv, w_f, b_f, cache_k, cache_v, cache_lf, dims):
    bp, lp, bs, ls, past = dims
    hp = N_HEADS // 2
    ntp = pl.cdiv(lp, KV_TILE)
    wf = jnp.pad(w_f, ((0, 0), (0, LANES - N_HEADS))).astype(BF16)
    wft = w_f.T.astype(BF16)
    bfr = jnp.pad(b_f, (0, LANES - N_HEADS)).reshape(1, LANES)
    bfc = b_f.reshape(N_HEADS, 1)
    whole = lambda a: pl.BlockSpec(a.shape, lambda b: (0,) * a.ndim)
    wspecs = [whole(wf), whole(wft), whole(bfr), whole(bfc)]
    lf_p, cum_p, cumt_p = pl.pallas_call(
        functools.partial(_fox_gate_prompt_kernel, seq=lp),
        grid=(bp,),
        in_specs=[pl.BlockSpec((lp, D_MODEL), lambda b: (b, 0))] + wspecs,
        out_specs=[pl.BlockSpec((lp, LANES), lambda b: (b, 0)),
                   pl.BlockSpec((lp, LANES), lambda b: (b, 0)),
                   pl.BlockSpec((1, ntp, N_HEADS, KV_TILE), lambda b: (b, 0, 0, 0))],
        out_shape=[jax.ShapeDtypeStruct((bp * lp, LANES), F32),
                   jax.ShapeDtypeStruct((bp * lp, LANES), F32),
                   jax.ShapeDtypeStruct((bp, ntp, N_HEADS, KV_TILE), F32)],
        compiler_params=_cparams(("arbitrary",), VMEM_LIMIT),
        name="fox_gate_prompt",
    )(hb, wf, wft, bfr, bfc)
    nts = pl.cdiv(past, KV_TILE)
    plft = jnp.pad(jnp.swapaxes(cache_lf, 1, 2), ((0, 0), (0, 0), (0, nts * KV_TILE - past)))
    row0 = bp * lp // ls
    lf_s, cum_s, cumt_past, cumt_new = pl.pallas_call(
        _fox_gate_sample_kernel,
        grid=(bs,),
        in_specs=[pl.BlockSpec((ls, D_MODEL), lambda b: (row0 + b, 0)),
                  pl.BlockSpec((1, N_HEADS, nts * KV_TILE), lambda b: (b, 0, 0))] + wspecs,
        out_specs=[pl.BlockSpec((ls, LANES), lambda b: (b, 0)),
                   pl.BlockSpec((ls, LANES), lambda b: (b, 0)),
                   pl.BlockSpec((1, nts, N_HEADS, KV_TILE), lambda b: (b, 0, 0, 0)),
                   pl.BlockSpec((1, N_HEADS, SMALL_TILE), lambda b: (b, 0, 0))],
        out_shape=[jax.ShapeDtypeStruct((bs * ls, LANES), F32),
                   jax.ShapeDtypeStruct((bs * ls, LANES), F32),
                   jax.ShapeDtypeStruct((bs, nts, N_HEADS, KV_TILE), F32),
                   jax.ShapeDtypeStruct((bs, N_HEADS, SMALL_TILE), F32)],
        compiler_params=_cparams(("arbitrary",), VMEM_LIMIT),
        name="fox_gate_sample",
    )(hb, plft, wf, wft, bfr, bfc)

    o_p = pl.pallas_call(
        functools.partial(_fox_prompt_kernel, seq=lp),
        grid=(bp, hp),
        in_specs=list(_qkv_specs(lp, 0, hp)) + [
            pl.BlockSpec((lp, LANES), lambda b, h: (b, 0)),
            pl.BlockSpec((1, ntp, N_HEADS, KV_TILE), lambda b, h: (b, 0, 0, 0))],
        out_specs=pl.BlockSpec((lp, LANES), lambda b, h: (b, h)),
        out_shape=jax.ShapeDtypeStruct((bp * lp, N_HEADS * HEAD_DIM), BF16),
        compiler_params=_cparams(("arbitrary", "arbitrary"), VMEM_LIMIT),
        name="fox_attn_prompt",
    )(qkv, qkv, qkv, cum_p, cumt_p)
    cache_spec = pl.BlockSpec((1, past, LANES), lambda b, h: (b, 0, h))
    o_s = pl.pallas_call(
        functools.partial(_fox_sample_kernel, past=past),
        grid=(bs, hp),
        in_specs=list(_qkv_specs(ls, row0, hp)) + [
            cache_spec, cache_spec,
            pl.BlockSpec((ls, LANES), lambda b, h: (b, 0)),
            pl.BlockSpec((1, nts, N_HEADS, KV_TILE), lambda b, h: (b, 0, 0, 0)),
            pl.BlockSpec((1, N_HEADS, SMALL_TILE), lambda b, h: (b, 0, 0))],
        out_specs=pl.BlockSpec((ls, LANES), lambda b, h: (b, h)),
        out_shape=jax.ShapeDtypeStruct((bs * ls, N_HEADS * HEAD_DIM), BF16),
        compiler_params=_cparams(("arbitrary", "arbitrary"), VMEM_LIMIT),
        name="fox_attn_sample",
    )(qkv, qkv, qkv, cache_k, cache_v, cum_s, cumt_past, cumt_new)
    o = jnp.concatenate([o_p, o_s], axis=0)
    lf = jnp.concatenate([lf_p, lf_s], axis=0)[:, :N_HEADS]
    return o, lf


def _rms(x, g):
    return x * lax.rsqrt(jnp.mean(x * x, axis=-1, keepdims=True) + RMS_EPS) * g


def _rope_group(x, cos, sin):
    half = QK_ROPE // 2
    lane = _iota(x.shape, 1)
    from_hi = pltpu.roll(x, LANES - half, axis=1)
    from_lo = pltpu.roll(x, half, axis=1)
    rot = jnp.where(lane < half, -from_hi, from_lo)
    return x * cos + rot * sin


def _mla_prep_kernel(c1_ref, gq_ref, gkv_ref, wuq_ref, cos_ref, sin_ref,
                     qn_ref, qr_ref, c_ref, kr_ref):
    nope = N_HEADS * HEAD_DIM
    c1 = c1_ref[...]
    cos, sin = cos_ref[...], sin_ref[...]
    qa = _rms(c1[:, :Q_LORA], gq_ref[...]).astype(BF16)
    q = _dot(qa, wuq_ref[...])
    qn_ref[...] = q[:, :nope].astype(BF16)
    for h in range(N_HEADS):
        lo = nope + h * LANES
        qr_ref[:, h * LANES:(h + 1) * LANES] = _rope_group(q[:, lo:lo + LANES], cos, sin).astype(BF16)
    c_ref[...] = _rms(c1[:, Q_LORA:Q_LORA + KV_LORA], gkv_ref[...])
    kr_ref[...] = _rope_group(c1[:, Q_LORA + KV_LORA:], cos, sin)


def _mla_tile(ql, qr, cb, krb, m, l, acc, mask):
    s = (_dot_nt(ql, cb) + _dot_nt(qr, krb)) * ((HEAD_DIM + QK_ROPE) ** -0.5)
    return _softmax_step(s, cb, m, l, acc, mask)


def _mla_heads(qn_ref, qr_ref, wuk_ref, wuv_ref, o_ref, attend, tq):
    def cols(ref, lo):
        x = ref[..., lo:lo + LANES]
        return x.reshape(x.shape[-2:])

    pairs_per_group = MLA_HEAD_GROUP // 2
    for grp in range(N_HEADS // MLA_HEAD_GROUP):
        pairs = range(grp * pairs_per_group, (grp + 1) * pairs_per_group)
        qls, qrs = [], []
        for g in pairs:
            qm = _stack_heads(cols(qn_ref, g * LANES).astype(F32))
            qls.append(_dot(qm, wuk_ref[g]).astype(BF16))
            qrs += [cols(qr_ref, 2 * g * LANES), cols(qr_ref, (2 * g + 1) * LANES)]
        o_lat = attend(jnp.concatenate(qls, axis=0), jnp.concatenate(qrs, axis=0)).astype(BF16)
        for gi, g in enumerate(pairs):
            lo = 2 * gi * tq
            out = _dot(o_lat[lo:lo + tq], wuv_ref[2 * g]) + _dot(o_lat[lo + tq:lo + 2 * tq], wuv_ref[2 * g + 1])
            o_ref[..., g * LANES:(g + 1) * LANES] = out.reshape(o_ref.shape[:-1] + (LANES,)).astype(o_ref.dtype)


def _chunk_of(pos):
    return lax.shift_right_arithmetic(pos - N_META, CHUNK.bit_length() - 1)


def _mla_prompt_kernel(qn_ref, qr_ref, cc_ref, kr_ref, wuk_ref, wuv_ref, o_ref, *, seq):
    i = pl.program_id(1)
    tq = KV_TILE
    rows = MLA_HEAD_GROUP * tq
    qs = i * tq
    diag_ks = jnp.minimum(qs, seq - KV_TILE)
    edge_ks = jnp.minimum(qs + KV_TILE, seq - SMALL_TILE)
    q_chunk = _chunk_of(qs + jnp.bitwise_and(_iota((rows, 1), 0), tq - 1))

    def chunk_mask(ks, width, first_new):
        kpos = ks + _iota((1, width), 1)
        return (_chunk_of(kpos) <= q_chunk) & (kpos >= first_new)

    def tile(ql, qr, ks, width, carry, mask):
        return _mla_tile(ql, qr, _load_tile(cc_ref, ks, width, width, (0,)),
                         _load_tile(kr_ref, ks, width, width, (0,)), *carry, mask)

    def attend(ql, qr):
        carry = lax.fori_loop(0, i, lambda j, c: tile(ql, qr, j * KV_TILE, KV_TILE, c, None),
                              _softmax_init(rows, KV_LORA))
        carry = tile(ql, qr, diag_ks, KV_TILE, carry, chunk_mask(diag_ks, KV_TILE, qs))
        m, l, acc = tile(ql, qr, edge_ks, SMALL_TILE, carry, chunk_mask(edge_ks, SMALL_TILE, qs + KV_TILE))
        return acc / l

    _mla_heads(qn_ref, qr_ref, wuk_ref, wuv_ref, o_ref, attend, tq)


def _mla_sample_kernel(qn_ref, qr_ref, cn_ref, krn_ref, cc_ref, krc_ref, wuk_ref, wuv_ref, o_ref, *, past):
    nfull, tail = divmod(past, KV_TILE)
    tq = qn_ref.shape[0]
    rows = MLA_HEAD_GROUP * tq
    col = _iota((rows, SMALL_TILE), 1)
    n_new = cn_ref.shape[0]

    def attend(ql, qr):
        carry = _softmax_init(rows, KV_LORA)
        carry = _mla_tile(ql, qr, _load_tile(cn_ref, 0, n_new, SMALL_TILE),
                          _load_tile(krn_ref, 0, n_new, SMALL_TILE), *carry, col < n_new)
        if tail:
            carry = _mla_tile(ql, qr, _load_tile(cc_ref, nfull * KV_TILE, tail, SMALL_TILE, (0,)),
                              _load_tile(krc_ref, nfull * KV_TILE, tail, SMALL_TILE, (0,)), *carry, col < tail)

        def cache_tile(ks, n, carry):
            return _mla_tile(ql, qr, _load_tile(cc_ref, ks, n, n, (0,)), _load_tile(krc_ref, ks, n, n, (0,)),
                             *carry, None)

        n_wide = nfull // WIDE
        for j in range(n_wide * WIDE, nfull):
            carry = cache_tile(j * KV_TILE, KV_TILE, carry)
        m, l, acc = lax.fori_loop(0, n_wide, lambda jw, c: cache_tile(jw * WIDE * KV_TILE, WIDE * KV_TILE, c),
                                  carry)
        return acc / l

    _mla_heads(qn_ref, qr_ref, wuk_ref, wuv_ref, o_ref, attend, tq)


def _rope_tables(dims):
    bp, lp, bs, ls, past = dims
    half = QK_ROPE // 2
    pos = jnp.concatenate([jnp.tile(jnp.arange(lp, dtype=jnp.int32), bp),
                           jnp.tile(past + jnp.arange(ls, dtype=jnp.int32), bs)])
    inv = ROPE_THETA ** (-jnp.arange(half, dtype=F32) / half)
    ang = pos.astype(F32)[:, None] * inv
    pad = jnp.zeros((pos.shape[0], LANES - QK_ROPE), F32)
    cos = jnp.concatenate([jnp.cos(ang), jnp.cos(ang), pad], axis=1)
    sin = jnp.concatenate([jnp.sin(ang), jnp.sin(ang), pad], axis=1)
    return cos, sin


def _mla_attention(hb, w, cache_c, cache_kr, dims):
    bp, lp, bs, ls, past = dims
    n = hb.shape[0]
    nope = N_HEADS * HEAD_DIM
    w_dq, g_q, w_uq, w_dkv, g_kv, w_uk, w_uv = w
    w1 = jnp.concatenate([w_dq, w_dkv, jnp.zeros((D_MODEL, LANES - QK_ROPE), F32)], axis=1).astype(BF16)
    c1 = _matmul(hb, w1, name="mla_down_proj")
    wuq = w_uq.reshape(Q_LORA, N_HEADS, HEAD_DIM + QK_ROPE)
    wuq = jnp.concatenate([
        wuq[:, :, :HEAD_DIM].reshape(Q_LORA, nope),
        jnp.pad(wuq[:, :, HEAD_DIM:], ((0, 0), (0, 0), (0, LANES - QK_ROPE))).reshape(Q_LORA, N_HEADS * LANES),
    ], axis=1).astype(BF16)
    cos, sin = _rope_tables(dims)
    tm = _row_tile(n, 256)
    c1w = c1.shape[1]
    row = lambda wd: pl.BlockSpec((tm, wd), lambda i: (i, 0))
    whole = lambda a: pl.BlockSpec(a.shape, lambda i: (0,) * a.ndim)
    gq, gkv = g_q.reshape(1, Q_LORA), g_kv.reshape(1, KV_LORA)
    qn, qr, c, kr = pl.pallas_call(
        _mla_prep_kernel,
        grid=(n // tm,),
        in_specs=[row(c1w), whole(gq), whole(gkv), whole(wuq), row(LANES), row(LANES)],
        out_specs=[row(nope), row(N_HEADS * LANES), row(KV_LORA), row(LANES)],
        out_shape=[jax.ShapeDtypeStruct((n, nope), BF16),
                   jax.ShapeDtypeStruct((n, N_HEADS * LANES), BF16),
                   jax.ShapeDtypeStruct((n, KV_LORA), F32),
                   jax.ShapeDtypeStruct((n, LANES), F32)],
        compiler_params=_cparams(("arbitrary",), VMEM_LIMIT),
        name="mla_prep",
    )(c1, gq, gkv, wuq, cos, sin)

    wuk = jnp.transpose(w_uk, (1, 2, 0)).reshape(N_HEADS // 2, LANES, KV_LORA).astype(BF16)
    wuv = jnp.transpose(w_uv, (1, 0, 2))
    zeros = jnp.zeros_like(wuv)
    even = (jnp.arange(N_HEADS) % 2 == 0)[:, None, None]
    wuv = jnp.concatenate([jnp.where(even, wuv, zeros), jnp.where(even, zeros, wuv)], axis=2).astype(BF16)

    np_ = bp * lp
    seq3 = lambda a, wd: a[:np_].reshape(bp, lp, wd)
    ccb = seq3(c, KV_LORA).astype(BF16)
    krb = seq3(kr, LANES).astype(BF16)
    nq = pl.cdiv(lp, KV_TILE)
    wspec = [pl.BlockSpec(wuk.shape, lambda b, i: (0, 0, 0)), pl.BlockSpec(wuv.shape, lambda b, i: (0, 0, 0))]
    o_p = pl.pallas_call(
        functools.partial(_mla_prompt_kernel, seq=lp),
        grid=(bp, nq),
        in_specs=[pl.BlockSpec((1, KV_TILE, nope), lambda b, i: (b, i, 0)),
                  pl.BlockSpec((1, KV_TILE, N_HEADS * LANES), lambda b, i: (b, i, 0)),
                  pl.BlockSpec((1, lp, KV_LORA), lambda b, i: (b, 0, 0)),
                  pl.BlockSpec((1, lp, LANES), lambda b, i: (b, 0, 0))] + wspec,
        out_specs=pl.BlockSpec((1, KV_TILE, nope), lambda b, i: (b, i, 0)),
        out_shape=jax.ShapeDtypeStruct((bp, lp, nope), BF16),
        compiler_params=_cparams(("arbitrary", "arbitrary"), VMEM_LIMIT),
        name="mla_attn_prompt",
    )(seq3(qn, nope), seq3(qr, N_HEADS * LANES), ccb, krb, wuk, wuv)

    row0 = np_ // ls
    krc = jnp.pad(cache_kr, ((0, 0), (0, 0), (0, LANES - QK_ROPE))).astype(BF16)
    wspec1 = [pl.BlockSpec(wuk.shape, lambda b: (0, 0, 0)), pl.BlockSpec(wuv.shape, lambda b: (0, 0, 0))]
    o_s = pl.pallas_call(
        functools.partial(_mla_sample_kernel, past=past),
        grid=(bs,),
        in_specs=[pl.BlockSpec((ls, nope), lambda b: (row0 + b, 0)),
                  pl.BlockSpec((ls, N_HEADS * LANES), lambda b: (row0 + b, 0)),
                  pl.BlockSpec((ls, KV_LORA), lambda b: (row0 + b, 0)),
                  pl.BlockSpec((ls, LANES), lambda b: (row0 + b, 0)),
                  pl.BlockSpec((1, past, KV_LORA), lambda b: (b, 0, 0)),
                  pl.BlockSpec((1, past, LANES), lambda b: (b, 0, 0))] + wspec1,
        out_specs=pl.BlockSpec((ls, nope), lambda b: (b, 0)),
        out_shape=jax.ShapeDtypeStruct((bs * ls, nope), BF16),
        compiler_params=_cparams(("arbitrary",), VMEM_LIMIT),
        name="mla_attn_sample",
    )(qn, qr, c, kr, cache_c, krc, wuk, wuv)
    o = jnp.concatenate([o_p.reshape(np_, nope), o_s], axis=0)
    return o, c, kr[:, :QK_ROPE]


def _router_kernel(x_ref, w_ref, b_ref, idx_ref, gate_ref):
    xh, xm, _ = _split3(x_ref[...])
    wh, wm, _ = _split3(w_ref[...])
    lg = _dot(xh, wh) + (_dot(xh, wm) + _dot(xm, wh)) + b_ref[...]
    lane = _iota(lg.shape, 1)
    lane_f = lane.astype(F32)
    lg = jnp.where(lane < N_EXPERTS, lg, -jnp.inf)
    vals, idxs = [], []
    for _ in range(TOP_K):
        m = jnp.max(lg, axis=-1, keepdims=True)
        sel = jnp.min(jnp.where(lg == m, lane_f, float(LANES)), axis=-1, keepdims=True)
        vals.append(m)
        idxs.append(sel)
        lg = jnp.where(lane_f == sel, -jnp.inf, lg)
    es = [jnp.exp(v - vals[0]) for v in vals]
    tot = es[0] + es[1] + es[2] + es[3]
    idx = jnp.zeros(lg.shape, F32)
    gate = jnp.zeros(lg.shape, F32)
    for k in range(TOP_K):
        idx = jnp.where(lane == k, idxs[k], idx)
        gate = jnp.where(lane == k, es[k] / tot, gate)
    idx_ref[...] = idx
    gate_ref[...] = gate


def _router(h, w_r, b_r):
    M, D = h.shape
    tm = _row_tile(M, 512)
    w = jnp.pad(w_r, ((0, 0), (0, LANES - N_EXPERTS)))
    b = jnp.pad(b_r, (0, LANES - N_EXPERTS)).reshape(1, LANES)
    row = pl.BlockSpec((tm, LANES), lambda i: (i, 0))
    return pl.pallas_call(
        _router_kernel,
        grid=(M // tm,),
        in_specs=[pl.BlockSpec((tm, D), lambda i: (i, 0)),
                  pl.BlockSpec((D, LANES), lambda i: (0, 0)),
                  pl.BlockSpec((1, LANES), lambda i: (0, 0))],
        out_specs=[row, row],
        out_shape=[jax.ShapeDtypeStruct((M, LANES), F32), jax.ShapeDtypeStruct((M, LANES), F32)],
        compiler_params=_cparams(("arbitrary",)),
        name="moe_router",
    )(h, w, b)


def _slot_kernel(idx_ref, dest_ref, cnt_ref, run_ref, start_ref):
    phase, t = pl.program_id(0), pl.program_id(1)
    idx = idx_ref[...]
    rows = idx.shape[0]
    lane = _iota((rows, LANES), 1).astype(F32)
    onehots = [jnp.where(lane == _lane_pick(idx, k), 1.0, 0.0) for k in range(TOP_K)]

    @pl.when((phase == 0) & (t == 0))
    def _():
        run_ref[...] = jnp.zeros_like(run_ref)

    @pl.when(phase == 0)
    def _():
        tot = onehots[0] + onehots[1] + onehots[2] + onehots[3]
        run_ref[...] += jnp.sum(tot, axis=0, keepdims=True)

    @pl.when((phase == 1) & (t == 0))
    def _():
        counts = run_ref[...]
        cnt_ref[...] = jnp.broadcast_to(counts, cnt_ref.shape)
        padded = jnp.floor((counts + (MOE_TILE - 1)) * (1.0 / MOE_TILE)) * MOE_TILE
        hi, mid, lo = _split3(jnp.broadcast_to(padded, (8, LANES)))
        tri = _tri(LANES, "lt")
        start_ref[...] = (_dot(hi, tri) + _dot(mid, tri) + _dot(lo, tri))[0:1]
        run_ref[...] = jnp.zeros_like(run_ref)

    @pl.when(phase == 1)
    def _():
        tri = _tri(rows, "ge")
        run = run_ref[...]
        out = jnp.zeros((rows, LANES), F32)
        for k, oh in enumerate(onehots):
            cum = _dot(tri, oh.astype(BF16)) + run
            slot = jnp.sum(oh * (start_ref[...] + cum - 1.0), axis=-1, keepdims=True)
            out = jnp.where(lane == k, slot, out)
            run = run + jnp.sum(oh, axis=0, keepdims=True)
        run_ref[...] = run
        dest_ref[...] = out.astype(jnp.int32)


def _slots(idx):
    M = idx.shape[0]
    tm = _row_tile(M, 512)
    return pl.pallas_call(
        _slot_kernel,
        grid=(2, M // tm),
        in_specs=[pl.BlockSpec((tm, LANES), lambda p, t: (t, 0))],
        out_specs=[pl.BlockSpec((tm, LANES), lambda p, t: (t * p, 0)),
                   pl.BlockSpec((8, LANES), lambda p, t: (0, 0))],
        out_shape=[jax.ShapeDtypeStruct((M, LANES), jnp.int32), jax.ShapeDtypeStruct((8, LANES), F32)],
        scratch_shapes=[pltpu.VMEM((1, LANES), F32), pltpu.VMEM((1, LANES), F32)],
        compiler_params=_cparams(("arbitrary", "arbitrary")),
        name="moe_slots",
    )(idx)


def _expert_kernel(te_ref, nt_ref, x_ref, wgu_ref, bgu_ref, wdn_ref, bdn_ref, o_ref, wgu_bf, wdn_bf):
    t = pl.program_id(0)

    @pl.when(t < nt_ref[0])
    def _():
        prev = te_ref[jnp.maximum(t - 1, 0)]

        @pl.when((t == 0) | (te_ref[t] != prev))
        def _():
            wgu_bf[...] = wgu_ref[0, 0].astype(BF16)
            wdn_bf[...] = wdn_ref[0, 0].astype(BF16)

        gu = _dot(x_ref[...].astype(BF16), wgu_bf[...]) + bgu_ref[0, 0]
        gate = jnp.minimum(gu[:, :D_EXPERT], SWIGLU_LIMIT)
        up = jnp.clip(gu[:, D_EXPERT:], -SWIGLU_LIMIT, SWIGLU_LIMIT)
        act = (up + 1.0) * gate * jax.nn.sigmoid(gate * SWIGLU_ALPHA)
        o_ref[...] = _dot(act.astype(BF16), wdn_bf[...]) + bdn_ref[0, 0]


def _moe(h, layer, w_r, b_r, w_gu, b_gu, w_dn, b_dn):
    n, d = h.shape
    tm = MOE_TILE
    idx, gates = _router(h, w_r, b_r)
    dest, counts = _slots(idx)
    dest = dest[:, :TOP_K]
    counts = counts[0, :N_EXPERTS].astype(jnp.int32)
    ends = jnp.cumsum(((counts + tm - 1) // tm) * tm)
    n_tiles = (n * TOP_K + N_EXPERTS * (tm - 1)) // tm + 1
    slots = n_tiles * tm
    used = (ends[-1] // tm).astype(jnp.int32)
    tile_start = jnp.minimum(jnp.arange(n_tiles, dtype=jnp.int32), used - 1) * tm
    tile_e = jnp.sum((ends[None, :] <= tile_start[:, None]).astype(jnp.int32), axis=1)
    token = jnp.zeros((slots,), jnp.int32).at[dest.reshape(-1)].set(
        jnp.arange(n * TOP_K, dtype=jnp.int32) // TOP_K, mode="promise_in_bounds", unique_indices=True)
    xs = h.at[token].get(mode="promise_in_bounds")

    tile = lambda t, te, nt: (jnp.minimum(t, nt[0] - 1), 0)
    expert = lambda t, te, nt: (layer, te[t], 0, 0)
    y = pl.pallas_call(
        _expert_kernel,
        grid_spec=pltpu.PrefetchScalarGridSpec(
            num_scalar_prefetch=2,
            grid=(n_tiles,),
            in_specs=[pl.BlockSpec((tm, d), tile),
                      pl.BlockSpec((1, 1, d, 2 * D_EXPERT), expert),
                      pl.BlockSpec((1, 1, 1, 2 * D_EXPERT), expert),
                      pl.BlockSpec((1, 1, D_EXPERT, d), expert),
                      pl.BlockSpec((1, 1, 1, d), expert)],
            out_specs=pl.BlockSpec((tm, d), tile),
            scratch_shapes=[pltpu.VMEM((d, 2 * D_EXPERT), BF16), pltpu.VMEM((D_EXPERT, d), BF16)]),
        out_shape=jax.ShapeDtypeStruct((slots, d), F32),
        compiler_params=_cparams(("arbitrary",), VMEM_LIMIT),
        name="moe_experts",
    )(tile_e, used.reshape(1), xs, w_gu, b_gu.reshape(b_gu.shape[:2] + (1, -1)),
      w_dn, b_dn.reshape(b_dn.shape[:2] + (1, -1)))
    ys = [y.at[dest[:, k]].get(mode="promise_in_bounds") for k in range(TOP_K)]
    return ys, gates


def kernel(x_prompt, x_sample, cache_sb_k, cache_sb_v, cache_fox_k, cache_fox_v, cache_fox_logf,
           cache_mla_ckv, cache_mla_krope, meta_tokens, a_w_qkv, a_w_o, b_w_qkv, b_w_f, b_b_f, b_w_o,
           c_w_dq, c_g_q, c_w_uq, c_w_dkv, c_g_kv, c_w_uk, c_w_uv, c_w_o, ln1_g, ln1_b, ln2_g, ln2_b,
           moe_w_router, moe_b_router, moe_w_gate_up, moe_b_gate_up, moe_w_down, moe_b_down):
    bp, sp, d = x_prompt.shape
    bs, ls, _ = x_sample.shape
    lp = N_META + sp
    past = cache_sb_k.shape[2]
    dims = (bp, lp, bs, ls, past)
    np_ = bp * lp
    hd = N_HEADS * HEAD_DIM

    meta = jnp.broadcast_to(meta_tokens[None], (bp, N_META, d))
    h = jnp.concatenate([jnp.concatenate([meta, x_prompt], axis=1).reshape(np_, d),
                         x_sample.reshape(bs * ls, d)], axis=0)
    hb = h.astype(BF16)
    new = {k: [] for k in ("a_k", "a_v", "b_k", "b_v", "b_lf", "c_c", "c_kr")}

    def heads(x, lo):
        return x[:, lo:lo + hd].reshape(-1, N_HEADS, HEAD_DIM)

    for i in range(DEPTH):
        kind, j = i % 3, i // 3
        if kind == 0:
            qkv = _matmul(hb, a_w_qkv[j].astype(BF16), name="sb_qkv_proj", tn=hd)
            o = _sb_attention(qkv, cache_sb_k[j].reshape(bs, past, hd), cache_sb_v[j].reshape(bs, past, hd), dims)
            y = _matmul(o, a_w_o[j].astype(BF16), name="sb_out_proj")
            new["a_k"].append(heads(qkv, hd))
            new["a_v"].append(heads(qkv, 2 * hd))
        elif kind == 1:
            qkv = _matmul(hb, b_w_qkv[j].astype(BF16), name="fox_qkv_proj", tn=hd)
            o, lf = _fox_attention(hb, qkv, b_w_f[j], b_b_f[j], cache_fox_k[j].reshape(bs, past, hd),
                                   cache_fox_v[j].reshape(bs, past, hd), cache_fox_logf[j], dims)
            y = _matmul(o, b_w_o[j].astype(BF16), name="fox_out_proj")
            new["b_k"].append(heads(qkv, hd))
            new["b_v"].append(heads(qkv, 2 * hd))
            new["b_lf"].append(lf)
        else:
            o, c, kr = _mla_attention(hb, (c_w_dq[j], c_g_q[j], c_w_uq[j], c_w_dkv[j], c_g_kv[j],
                                           c_w_uk[j], c_w_uv[j]), cache_mla_ckv[j], cache_mla_krope[j], dims)
            y = _matmul(o, c_w_o[j].astype(BF16), name="mla_out_proj")
            new["c_c"].append(c)
            new["c_kr"].append(kr)
        h, hb = _deepnorm_ln(h, [y], None, ln1_g[i], ln1_b[i], name="ln_mixer")
        ys, gates = _moe(h, i, moe_w_router[i], moe_b_router[i], moe_w_gate_up, moe_b_gate_up,
                         moe_w_down, moe_b_down)
        h, hb = _deepnorm_ln(h, ys, gates, ln2_g[i], ln2_b[i], name="ln_moe")

    def split(rows, tail_shape):
        xs = jnp.stack(rows)
        return (xs[:, :np_].reshape((len(rows), bp, lp) + tail_shape),
                xs[:, np_:].reshape((len(rows), bs, ls) + tail_shape))

    outs_p, outs_s = [], []
    for key, tail_shape in (("a_k", (N_HEADS, HEAD_DIM)), ("a_v", (N_HEADS, HEAD_DIM)),
                            ("b_k", (N_HEADS, HEAD_DIM)), ("b_v", (N_HEADS, HEAD_DIM)),
                            ("b_lf", (N_HEADS,)), ("c_c", (KV_LORA,)), ("c_kr", (QK_ROPE,))):
        p, s = split(new[key], tail_shape)
        outs_p.append(p)
        outs_s.append(s)
    y_prompt = h[:np_].reshape(bp, lp, d)[:, N_META:]
    y_sample = h[np_:].reshape(bs, ls, d)
    return (y_prompt, y_sample) + tuple(outs_p) + tuple(outs_s)
```

```python
import functools

import jax
import jax.numpy as jnp
from jax import lax
from jax.experimental import pallas as pl
from jax.experimental.pallas import tpu as pltpu

F32 = jnp.float32
BF16 = jnp.bfloat16

D_MODEL = 1024
N_META = 16
CHUNK = 64
N_HEADS = 16
HEAD_DIM = 64
Q_LORA = 384
KV_LORA = 256
QK_ROPE = 32
ROPE_THETA = 10000.0
N_EXPERTS = 32
TOP_K = 4
D_EXPERT = 1024
SWIGLU_LIMIT = 7.0
SWIGLU_ALPHA = 1.702
DEPTH = 4
DN_ALPHA = (2 * DEPTH) ** 0.25
LN_EPS = 1e-5
RMS_EPS = 1e-6

LANES = 128
KV_TILE = 256
SMALL_TILE = 128
Q_BLOCK = 2 * KV_TILE
WIDE = 4
MLA_HEAD_GROUP = 4
MOE_TILE = 256
NEG = -1e30
VMEM_LIMIT = 56 * 2 ** 20


def _cparams(sem, vmem=None):
    return pltpu.CompilerParams(dimension_semantics=sem, vmem_limit_bytes=vmem)


def _row_tile(n, cap, mult=16):
    t = cap - cap % mult
    while t > mult and n % t:
        t -= mult
    assert n % t == 0, (n, cap)
    return t


def _dot(a, b):
    return jnp.dot(a, b, preferred_element_type=F32)


def _dot_nt(a, b):
    return lax.dot_general(a, b, (((1,), (1,)), ((), ())), preferred_element_type=F32)


def _split3(x):
    hi = x.astype(BF16)
    r = x - hi.astype(F32)
    mid = r.astype(BF16)
    lo = (r - mid.astype(F32)).astype(BF16)
    return hi, mid, lo


def _iota(shape, axis):
    return lax.broadcasted_iota(jnp.int32, shape, axis)


def _tri(n, op):
    r, c = _iota((n, n), 0), _iota((n, n), 1)
    m = {"gt": r > c, "ge": r >= c, "le": r <= c, "lt": r < c}[op]
    return jnp.where(m, 1.0, 0.0).astype(BF16)


def _pad_rows(x, n):
    if x.shape[0] == n:
        return x
    return jnp.concatenate([x, jnp.zeros((n - x.shape[0],) + x.shape[1:], x.dtype)], axis=0)


def _pad_lanes(x, n):
    if x.shape[1] == n:
        return x
    return jnp.concatenate([x, jnp.zeros((x.shape[0], n - x.shape[1]), x.dtype)], axis=1)


def _lane_pick(x, k):
    return jnp.sum(jnp.where(_iota(x.shape, 1) == k, x, 0), axis=-1, keepdims=True)


def _mm_kernel(x_ref, w_ref, o_ref):
    o_ref[...] = _dot(x_ref[...], w_ref[...]).astype(o_ref.dtype)


def _matmul(x, w, *, name, tn=None, out_dtype=F32):
    M, K = x.shape
    N = w.shape[1]
    tn = N if tn is None else tn
    tm = _row_tile(M, 512)
    return pl.pallas_call(
        _mm_kernel,
        grid=(N // tn, M // tm),
        in_specs=[pl.BlockSpec((tm, K), lambda j, i: (i, 0)),
                  pl.BlockSpec((K, tn), lambda j, i: (0, j))],
        out_specs=pl.BlockSpec((tm, tn), lambda j, i: (i, j)),
        out_shape=jax.ShapeDtypeStruct((M, N), out_dtype),
        compiler_params=_cparams(("arbitrary", "arbitrary")),
        name=name,
    )(x, w)


def _layer_norm(x, g, b):
    mu = jnp.mean(x, axis=-1, keepdims=True)
    xc = x - mu
    var = jnp.mean(xc * xc, axis=-1, keepdims=True)
    return xc * lax.rsqrt(var + LN_EPS) * g + b


def _ln_kernel(h_ref, y_ref, g_ref, b_ref, o_ref, ob_ref):
    o = _layer_norm(DN_ALPHA * h_ref[...] + y_ref[...], g_ref[...], b_ref[...])
    o_ref[...] = o
    ob_ref[...] = o.astype(BF16)


def _ln_combine_kernel(h_ref, y0_ref, y1_ref, y2_ref, y3_ref, gate_ref, g_ref, b_ref, o_ref, ob_ref):
    gate = gate_ref[...]
    f = None
    for k, y_ref in enumerate((y0_ref, y1_ref, y2_ref, y3_ref)):
        term = _lane_pick(gate, k) * y_ref[...]
        f = term if f is None else f + term
    o = _layer_norm(DN_ALPHA * h_ref[...] + f, g_ref[...], b_ref[...])
    o_ref[...] = o
    ob_ref[...] = o.astype(BF16)


def _deepnorm_ln(h, ys, gates, g, b, *, name):
    M, D = h.shape
    tm = _row_tile(M, 512)
    row = pl.BlockSpec((tm, D), lambda i: (i, 0))
    vec = pl.BlockSpec((1, D), lambda i: (0, 0))
    extra = [] if gates is None else [pl.BlockSpec((tm, LANES), lambda i: (i, 0))]
    args = list(ys) + ([] if gates is None else [gates])
    return pl.pallas_call(
        _ln_kernel if gates is None else _ln_combine_kernel,
        grid=(M // tm,),
        in_specs=[row] + [row] * len(ys) + extra + [vec, vec],
        out_specs=[row, row],
        out_shape=[jax.ShapeDtypeStruct((M, D), F32), jax.ShapeDtypeStruct((M, D), BF16)],
        compiler_params=_cparams(("arbitrary",)),
        name=name,
    )(h, *args, g.reshape(1, D), b.reshape(1, D))


def _stack_heads(q):
    lane = _iota(q.shape, 1)
    q0 = jnp.where(lane < HEAD_DIM, q, 0.0).astype(BF16)
    q1 = jnp.where(lane >= HEAD_DIM, q, 0.0).astype(BF16)
    return jnp.concatenate([q0, q1], axis=0)


def _unstack_heads(acc, tq):
    lane = _iota((tq, LANES), 1)
    return jnp.where(lane < HEAD_DIM, acc[:tq], acc[tq:])


def _stacked_rows(n_stack, tq, width):
    row = _iota((n_stack * tq, width), 0)
    for _ in range(n_stack - 1):
        row = jnp.where(row >= tq, row - tq, row)
    return row


def _softmax_init(rows, width):
    return (jnp.full((rows, 1), NEG, F32), jnp.zeros((rows, 1), F32), jnp.zeros((rows, width), F32))


def _softmax_step(u, vb, m, l, acc, mask):
    if mask is not None:
        u = jnp.where(mask, u, NEG)
    m_new = jnp.maximum(m, jnp.max(u, axis=-1, keepdims=True))
    p = jnp.exp(u - m_new)
    if mask is not None:
        p = jnp.where(mask, p, 0.0)
    alpha = jnp.exp(m - m_new)
    l = alpha * l + jnp.sum(p, axis=-1, keepdims=True)
    acc = alpha * acc + _dot(p.astype(BF16), vb)
    return m_new, l, acc


def _qkv_specs(seq, row_block0, n_col_blocks):
    def spec(part):
        return pl.BlockSpec((seq, LANES), lambda b, hp: (row_block0 + b, part * n_col_blocks + hp))
    return spec(0), spec(1), spec(2)


SKIP, MASKED, FULL = "skip", "masked", "full"


def _prompt_blocks(seq, q_block, split):
    nfull, tail = divmod(seq, KV_TILE)
    assert nfull % 2 == 0 and nfull >= 2, seq
    nblk = nfull // 2

    def parts_of(extra):
        return [(0, KV_TILE), (KV_TILE, KV_TILE + extra)] if split else [(0, Q_BLOCK + extra)]

    def modes(first, second):
        return (first, second) if split else (MASKED,)

    def diag_of(qs):
        return [(qs + KV_TILE, KV_TILE, KV_TILE, modes(SKIP, MASKED)), (qs, KV_TILE, KV_TILE, modes(MASKED, FULL))]

    def body(i, carry):
        qs = pl.multiple_of(i * Q_BLOCK, Q_BLOCK)
        q_block(qs, parts_of(0), diag_of(qs), 2 * i)
        return carry

    lax.fori_loop(0, nblk - 1, body, 0)
    qs = (nblk - 1) * Q_BLOCK
    diag = diag_of(qs)
    if tail:
        diag = [(nfull * KV_TILE, tail, SMALL_TILE, modes(SKIP, MASKED))] + diag
    q_block(qs, parts_of(tail), diag, 2 * (nblk - 1))


def _causal_mask(ks, width, q0, rows, strict):
    kpos = ks + _iota((2 * rows, width), 1)
    qpos = q0 + _stacked_rows(2, rows, width)
    return kpos < qpos if strict else kpos <= qpos


def _rows_at(start, n):
    if not isinstance(start, int):
        start = pl.multiple_of(start, 16)
    return pl.ds(start, n)


def _load_tile(ref, ks, n, width, lead=()):
    return _pad_rows(ref[lead + (_rows_at(ks, n), slice(None))].astype(BF16), width)


def _sb_scores(qm, kb):
    s = _dot_nt(qm, kb)
    ls = jnp.minimum(s, 0.0) - jnp.log(1.0 + jnp.exp(-jnp.abs(s)))
    return ls, ls - s


def _sb_tile(qm, kb, vb, r, acc, mask, tri_gt):
    return _sb_accumulate(_sb_scores(qm, kb), vb, r, acc, mask, tri_gt)


def _sb_accumulate(scores, vb, r, acc, mask, tri_gt):
    ls, l = scores
    if mask is not None:
        l = jnp.where(mask, l, 0.0)
    w = tri_gt.shape[0]
    parts = []
    for lo in reversed(range(0, l.shape[1], w)):
        lb = l[:, lo:lo + w]
        lhi = lb.astype(BF16)
        llo = (lb - lhi.astype(F32)).astype(BF16)
        c2 = _dot(jnp.concatenate([lhi, llo], axis=0), tri_gt)
        c = c2[:lb.shape[0]] + c2[lb.shape[0]:]
        parts.append(jnp.exp(ls[:, lo:lo + w] + c + r))
        r = r + jnp.sum(lb, axis=-1, keepdims=True)
    a = parts[0] if len(parts) == 1 else jnp.concatenate(parts[::-1], axis=1)
    if mask is not None:
        a = jnp.where(mask, a, 0.0)
    acc = acc + _dot(a.astype(BF16), vb)
    return r, acc


def _sb_prompt_kernel(q_ref, k_ref, v_ref, o_ref, *, seq):
    tris = {KV_TILE: _tri(KV_TILE, "gt"), SMALL_TILE: _tri(SMALL_TILE, "gt")}
    scale = HEAD_DIM ** -0.5

    def q_block(qs, halves, diag, n_left):
        qms = [_stack_heads(q_ref[_rows_at(qs + off, rows), :] * scale) for off, rows in halves]
        carry = [(jnp.zeros((2 * rows, 1), F32), jnp.zeros((2 * rows, LANES), F32)) for _, rows in halves]
        for ks, n, width, modes in diag:
            kb, vb = _load_tile(k_ref, ks, n, width), _load_tile(v_ref, ks, n, width)
            for h, ((off, rows), mode) in enumerate(zip(halves, modes)):
                if mode != SKIP:
                    mask = _causal_mask(ks, width, qs + off, rows, True) if mode == MASKED else None
                    carry[h] = _sb_tile(qms[h], kb, vb, *carry[h], mask, tris[width])

        def body(jj, carry):
            ks = (n_left - 1 - jj) * KV_TILE
            kb, vb = _load_tile(k_ref, ks, KV_TILE, KV_TILE), _load_tile(v_ref, ks, KV_TILE, KV_TILE)
            return tuple(_sb_tile(qm, kb, vb, *c, None, tris[KV_TILE]) for qm, c in zip(qms, carry))

        carry = lax.fori_loop(0, n_left, body, tuple(carry))
        for (off, rows), (r, acc) in zip(halves, carry):
            o_ref[_rows_at(qs + off, rows), :] = _unstack_heads(acc, rows).astype(o_ref.dtype)

    _prompt_blocks(seq, q_block, split=False)


def _sb_sample_kernel(q_ref, kn_ref, vn_ref, kc_ref, vc_ref, o_ref, *, past):
    nfull, tail = divmod(past, KV_TILE)
    tq = q_ref.shape[0]
    tri_full, tri_small = _tri(KV_TILE, "gt"), _tri(SMALL_TILE, "gt")
    qm = _stack_heads(q_ref[...] * (HEAD_DIM ** -0.5))
    row, col = _stacked_rows(2, tq, SMALL_TILE), _iota((2 * tq, SMALL_TILE), 1)
    r = jnp.zeros((2 * tq, 1), F32)
    acc = jnp.zeros((2 * tq, LANES), F32)
    n_new = kn_ref.shape[0]
    r, acc = _sb_tile(qm, _load_tile(kn_ref, 0, n_new, SMALL_TILE), _load_tile(vn_ref, 0, n_new, SMALL_TILE),
                      r, acc, col < row, tri_small)
    if tail:
        r, acc = _sb_tile(qm, _load_tile(kc_ref, nfull * KV_TILE, tail, SMALL_TILE, (0,)),
                          _load_tile(vc_ref, nfull * KV_TILE, tail, SMALL_TILE, (0,)),
                          r, acc, col < tail, tri_small)

    def cache_tile(ks, n, carry):
        return _sb_tile(qm, _load_tile(kc_ref, ks, n, n, (0,)), _load_tile(vc_ref, ks, n, n, (0,)),
                        carry[0], carry[1], None, tri_full)

    n_wide = nfull // WIDE
    carry = (r, acc)
    for j in reversed(range(n_wide * WIDE, nfull)):
        carry = cache_tile(j * KV_TILE, KV_TILE, carry)
    r, acc = lax.fori_loop(0, n_wide, lambda jj, c: cache_tile((n_wide - 1 - jj) * WIDE * KV_TILE,
                                                                 WIDE * KV_TILE, c), carry)
    o_ref[...] = _unstack_heads(acc, tq).astype(o_ref.dtype)


def _sb_attention(qkv, cache_k, cache_v, layer, dims):
    bp, lp, bs, ls, past = dims
    hp = N_HEADS // 2
    o_p = pl.pallas_call(
        functools.partial(_sb_prompt_kernel, seq=lp),
        grid=(bp, hp),
        in_specs=list(_qkv_specs(lp, 0, hp)),
        out_specs=pl.BlockSpec((lp, LANES), lambda b, h: (b, h)),
        out_shape=jax.ShapeDtypeStruct((bp * lp, N_HEADS * HEAD_DIM), BF16),
        compiler_params=_cparams(("arbitrary", "arbitrary"), VMEM_LIMIT),
        name="sb_attn_prompt",
    )(qkv, qkv, qkv)
    cache_spec = pl.BlockSpec((None, 1, past, LANES), lambda b, h: (layer, b, 0, h))
    o_s = pl.pallas_call(
        functools.partial(_sb_sample_kernel, past=past),
        grid=(bs, hp),
        in_specs=list(_qkv_specs(ls, bp * lp // ls, hp)) + [cache_spec, cache_spec],
        out_specs=pl.BlockSpec((ls, LANES), lambda b, h: (b, h)),
        out_shape=jax.ShapeDtypeStruct((bs * ls, N_HEADS * HEAD_DIM), BF16),
        compiler_params=_cparams(("arbitrary", "arbitrary"), VMEM_LIMIT),
        name="sb_attn_sample",
    )(qkv, qkv, qkv, cache_k, cache_v)
    return jnp.concatenate([o_p, o_s], axis=0)


def _log_sigmoid(x):
    return jnp.minimum(x, 0.0) - jnp.log1p(jnp.exp(-jnp.abs(x)))


def _cumsum_rows(x, tri_ge):
    hi, mid, lo = _split3(x)
    return _dot(tri_ge, hi) + _dot(tri_ge, mid) + _dot(tri_ge, lo)


def _cumsum_lanes(x, tri_le):
    hi, mid, lo = _split3(x)
    return _dot(hi, tri_le) + _dot(mid, tri_le) + _dot(lo, tri_le)


def _gate_chunk(hc, valid, wf_ref, wft_ref, bf_row_ref, bf_col_ref):
    n = hc.shape[0]
    g = _dot(hc, wf_ref[...]) + bf_row_ref[...]
    gt = _dot_nt(wft_ref[...], hc) + bf_col_ref[...]
    lf, lft = _log_sigmoid(g), _log_sigmoid(gt)
    if valid < n:
        lf = jnp.where(_iota(lf.shape, 0) < valid, lf, 0.0)
        lft = jnp.where(_iota(lft.shape, 1) < valid, lft, 0.0)
    return lf, lft


def _fox_gate_prompt_kernel(h_ref, wf_ref, wft_ref, bfr_ref, bfc_ref, lf_ref, cum_ref, cumt_ref, *, seq):
    nfull, tail = divmod(seq, KV_TILE)
    carry = jnp.zeros((1, LANES), F32)
    carry_t = jnp.zeros((N_HEADS, 1), F32)
    for j in range(nfull + (1 if tail else 0)):
        n = KV_TILE if j < nfull else SMALL_TILE
        valid = KV_TILE if j < nfull else tail
        hc = _pad_rows(h_ref[j * KV_TILE:j * KV_TILE + valid, :], n)
        lf, lft = _gate_chunk(hc, valid, wf_ref, wft_ref, bfr_ref, bfc_ref)
        cum = _cumsum_rows(lf, _tri(n, "ge")) + carry
        cum_t = _cumsum_lanes(lft, _tri(n, "le")) + carry_t
        lf_ref[j * KV_TILE:j * KV_TILE + valid, :] = lf[:valid]
        cum_ref[j * KV_TILE:j * KV_TILE + valid, :] = cum[:valid]
        cumt_ref[0, j] = _pad_lanes(cum_t, KV_TILE)
        carry = cum[valid - 1:valid, :]
        carry_t = cum_t[:, valid - 1:valid]


def _fox_gate_sample_kernel(h_ref, plft_ref, wf_ref, wft_ref, bfr_ref, bfc_ref,
                            lf_ref, cum_ref, cumt_past_ref, cumt_new_ref):
    ntiles = plft_ref.shape[2] // KV_TILE
    carry_t = jnp.zeros((N_HEADS, 1), F32)
    tri_le = _tri(KV_TILE, "le")
    for j in range(ntiles):
        cum_t = _cumsum_lanes(plft_ref[0, :, j * KV_TILE:(j + 1) * KV_TILE], tri_le) + carry_t
        cumt_past_ref[0, j] = cum_t
        carry_t = cum_t[:, KV_TILE - 1:KV_TILE]
    ls = h_ref.shape[0]
    hc = _pad_rows(h_ref[...], SMALL_TILE)
    lf, lft = _gate_chunk(hc, ls, wf_ref, wft_ref, bfr_ref, bfc_ref)
    eye = _iota((N_HEADS, LANES), 0) == _iota((N_HEADS, LANES), 1)
    carry = jnp.sum(jnp.where(eye, carry_t, 0.0), axis=0, keepdims=True)
    cum = _cumsum_rows(lf, _tri(SMALL_TILE, "ge")) + carry
    cum_t = _cumsum_lanes(lft, _tri(SMALL_TILE, "le")) + carry_t
    lf_ref[...] = lf[:ls]
    cum_ref[...] = cum[:ls]
    cumt_new_ref[0] = cum_t


def _fx_tile(qm, kb, vb, ck, cq, m, l, acc, mask):
    return _softmax_step(_fx_scores(qm, kb, ck, cq), vb, m, l, acc, mask)


def _fx_scores(qm, kb, ck, cq):
    s = _dot_nt(qm, kb)
    tq = s.shape[0] // 2
    return jnp.concatenate([(s[:tq] + cq[:tq]) - ck[0], (s[tq:] + cq[tq:]) - ck[1]], axis=0)


def _head_rows(ref, lead, head0, width):
    return tuple(ref[lead + (pl.ds(head0 + hh, 1), slice(None))][:, :width] for hh in range(2))


def _fox_prompt_kernel(q_ref, k_ref, v_ref, cum_ref, cumt_ref, o_ref, *, seq):
    head0 = 2 * pl.program_id(1)
    scale = HEAD_DIM ** -0.5

    def ck_rows(j, width):
        return _head_rows(cumt_ref, (0, j), head0, width)

    def q_block(qs, halves, diag, n_left):
        qms, cqs, carry = [], [], []
        for off, rows in halves:
            qms.append(_stack_heads(q_ref[_rows_at(qs + off, rows), :] * scale))
            cum_q = cum_ref[_rows_at(qs + off, rows), :]
            cqs.append(jnp.concatenate([_lane_pick(cum_q, head0), _lane_pick(cum_q, head0 + 1)], axis=0))
            carry.append(_softmax_init(2 * rows, LANES))
        for ks, n, width, modes in diag:
            kb, vb = _load_tile(k_ref, ks, n, width), _load_tile(v_ref, ks, n, width)
            ck = ck_rows(ks // KV_TILE, width)
            for h, ((off, rows), mode) in enumerate(zip(halves, modes)):
                if mode != SKIP:
                    mask = _causal_mask(ks, width, qs + off, rows, False) if mode == MASKED else None
                    carry[h] = _fx_tile(qms[h], kb, vb, ck, cqs[h], *carry[h], mask)

        def body(j, carry):
            ks = j * KV_TILE
            kb, vb = _load_tile(k_ref, ks, KV_TILE, KV_TILE), _load_tile(v_ref, ks, KV_TILE, KV_TILE)
            ck = ck_rows(j, KV_TILE)
            return tuple(_fx_tile(qm, kb, vb, ck, cq, *c, None) for qm, cq, c in zip(qms, cqs, carry))

        carry = lax.fori_loop(0, n_left, body, tuple(carry))
        for (off, rows), (m, l, acc) in zip(halves, carry):
            o_ref[_rows_at(qs + off, rows), :] = _unstack_heads(acc / l, rows).astype(o_ref.dtype)

    _prompt_blocks(seq, q_block, split=True)


def _fox_sample_kernel(q_ref, kn_ref, vn_ref, kc_ref, vc_ref, cum_ref, cumt_past_ref, cumt_new_ref,
                       o_ref, *, past):
    nfull, tail = divmod(past, KV_TILE)
    tq = q_ref.shape[0]
    head0 = 2 * pl.program_id(1)
    qm = _stack_heads(q_ref[...] * (HEAD_DIM ** -0.5))
    cum_q = cum_ref[...]
    cq = jnp.concatenate([_lane_pick(cum_q, head0), _lane_pick(cum_q, head0 + 1)], axis=0)
    row, col = _stacked_rows(2, tq, SMALL_TILE), _iota((2 * tq, SMALL_TILE), 1)
    n_new = kn_ref.shape[0]
    carry = _softmax_init(2 * tq, LANES)
    carry = _fx_tile(qm, _load_tile(kn_ref, 0, n_new, SMALL_TILE), _load_tile(vn_ref, 0, n_new, SMALL_TILE),
                     _head_rows(cumt_new_ref, (0,), head0, SMALL_TILE), cq, *carry, col <= row)
    if tail:
        carry = _fx_tile(qm, _load_tile(kc_ref, nfull * KV_TILE, tail, SMALL_TILE, (0,)),
                         _load_tile(vc_ref, nfull * KV_TILE, tail, SMALL_TILE, (0,)),
                         _head_rows(cumt_past_ref, (0, nfull), head0, SMALL_TILE), cq, *carry, col < tail)

    def cache_tile(j, n_tiles, carry):
        ks, n = j * KV_TILE, n_tiles * KV_TILE
        rows = [_head_rows(cumt_past_ref, (0, j + t), head0, KV_TILE) for t in range(n_tiles)]
        ck = tuple(rows[0][hh] if n_tiles == 1 else jnp.concatenate([r[hh] for r in rows], axis=1)
                   for hh in range(2))
        return _fx_tile(qm, _load_tile(kc_ref, ks, n, n, (0,)), _load_tile(vc_ref, ks, n, n, (0,)),
                        ck, cq, *carry, None)

    n_wide = nfull // WIDE
    for j in range(n_wide * WIDE, nfull):
        carry = cache_tile(j, 1, carry)
    m, l, acc = lax.fori_loop(0, n_wide, lambda jw, c: cache_tile(jw * WIDE, WIDE, c), carry)
    o_ref[...] = _unstack_heads(acc / l, tq).astype(o_ref.dtype)


def _fox_attention(hb, qkv, w_f, b_f, cache_k, cache_v, cache_lf, layer, dims):
    bp, lp, bs, ls, past = dims
    hp = N_HEADS // 2
    ntp = pl.cdiv(lp, KV_TILE)
    wf = jnp.pad(w_f, ((0, 0), (0, LANES - N_HEADS))).astype(BF16)
    wft = w_f.T.astype(BF16)
    bfr = jnp.pad(b_f, (0, LANES - N_HEADS)).reshape(1, LANES)
    bfc = b_f.reshape(N_HEADS, 1)
    whole = lambda a: pl.BlockSpec(a.shape, lambda b: (0,) * a.ndim)
    wspecs = [whole(wf), whole(wft), whole(bfr), whole(bfc)]
    lf_p, cum_p, cumt_p = pl.pallas_call(
        functools.partial(_fox_gate_prompt_kernel, seq=lp),
        grid=(bp,),
        in_specs=[pl.BlockSpec((lp, D_MODEL), lambda b: (b, 0))] + wspecs,
        out_specs=[pl.BlockSpec((lp, LANES), lambda b: (b, 0)),
                   pl.BlockSpec((lp, LANES), lambda b: (b, 0)),
                   pl.BlockSpec((1, ntp, N_HEADS, KV_TILE), lambda b: (b, 0, 0, 0))],
        out_shape=[jax.ShapeDtypeStruct((bp * lp, LANES), F32),
                   jax.ShapeDtypeStruct((bp * lp, LANES), F32),
                   jax.ShapeDtypeStruct((bp, ntp, N_HEADS, KV_TILE), F32)],
        compiler_params=_cparams(("arbitrary",), VMEM_LIMIT),
        name="fox_gate_prompt",
    )(hb, wf, wft, bfr, bfc)
    nts = pl.cdiv(past, KV_TILE)
    plft = jnp.pad(jnp.swapaxes(cache_lf, 1, 2), ((0, 0), (0, 0), (0, nts * KV_TILE - past)))
    row0 = bp * lp // ls
    lf_s, cum_s, cumt_past, cumt_new = pl.pallas_call(
        _fox_gate_sample_kernel,
        grid=(bs,),
        in_specs=[pl.BlockSpec((ls, D_MODEL), lambda b: (row0 + b, 0)),
                  pl.BlockSpec((1, N_HEADS, nts * KV_TILE), lambda b: (b, 0, 0))] + wspecs,
        out_specs=[pl.BlockSpec((ls, LANES), lambda b: (b, 0)),
                   pl.BlockSpec((ls, LANES), lambda b: (b, 0)),
                   pl.BlockSpec((1, nts, N_HEADS, KV_TILE), lambda b: (b, 0, 0, 0)),
                   pl.BlockSpec((1, N_HEADS, SMALL_TILE), lambda b: (b, 0, 0))],
        out_shape=[jax.ShapeDtypeStruct((bs * ls, LANES), F32),
                   jax.ShapeDtypeStruct((bs * ls, LANES), F32),
                   jax.ShapeDtypeStruct((bs, nts, N_HEADS, KV_TILE), F32),
                   jax.ShapeDtypeStruct((bs, N_HEADS, SMALL_TILE), F32)],
        compiler_params=_cparams(("arbitrary",), VMEM_LIMIT),
        name="fox_gate_sample",
    )(hb, plft, wf, wft, bfr, bfc)

    o_p = pl.pallas_call(
        functools.partial(_fox_prompt_kernel, seq=lp),
        grid=(bp, hp),
        in_specs=list(_qkv_specs(lp, 0, hp)) + [
            pl.BlockSpec((lp, LANES), lambda b, h: (b, 0)),
            pl.BlockSpec((1, ntp, N_HEADS, KV_TILE), lambda b, h: (b, 0, 0, 0))],
        out_specs=pl.BlockSpec((lp, LANES), lambda b, h: (b, h)),
        out_shape=jax.ShapeDtypeStruct((bp * lp, N_HEADS * HEAD_DIM), BF16),
        compiler_params=_cparams(("arbitrary", "arbitrary"), VMEM_LIMIT),
        name="fox_attn_prompt",
    )(qkv, qkv, qkv, cum_p, cumt_p)
    cache_spec = pl.BlockSpec((None, 1, past, LANES), lambda b, h: (layer, b, 0, h))
    o_s = pl.pallas_call(
        functools.partial(_fox_sample_kernel, past=past),
        grid=(bs, hp),
        in_specs=list(_qkv_specs(ls, row0, hp)) + [
            cache_spec, cache_spec,
            pl.BlockSpec((ls, LANES), lambda b, h: (b, 0)),
            pl.BlockSpec((1, nts, N_HEADS, KV_TILE), lambda b, h: (b, 0, 0, 0)),
            pl.BlockSpec((1, N_HEADS, SMALL_TILE), lambda b, h: (b, 0, 0))],
        out_specs=pl.BlockSpec((ls, LANES), lambda b, h: (b, h)),
        out_shape=jax.ShapeDtypeStruct((bs * ls, N_HEADS * HEAD_DIM), BF16),
        compiler_params=_cparams(("arbitrary", "arbitrary"), VMEM_LIMIT),
        name="fox_attn_sample",
    )(qkv, qkv, qkv, cache_k, cache_v, cum_s, cumt_past, cumt_new)
    o = jnp.concatenate([o_p, o_s], axis=0)
    lf = jnp.concatenate([lf_p, lf_s], axis=0)[:, :N_HEADS]
    return o, lf


def _rms(x, g):
    return x * lax.rsqrt(jnp.mean(x * x, axis=-1, keepdims=True) + RMS_EPS) * g


def _rope_group(x, cos, sin):
    half = QK_ROPE // 2
    lane = _iota(x.shape, 1)
    from_hi = pltpu.roll(x, LANES - half, axis=1)
    from_lo = pltpu.roll(x, half, axis=1)
    rot = jnp.where(lane < half, -from_hi, from_lo)
    return x * cos + rot * sin


def _mla_prep_kernel(c1_ref, gq_ref, gkv_ref, wuq_ref, cos_ref, sin_ref,
                     qn_ref, qr_ref, c_ref, kr_ref):
    nope = N_HEADS * HEAD_DIM
    c1 = c1_ref[...]
    cos, sin = cos_ref[...], sin_ref[...]
    qa = _rms(c1[:, :Q_LORA], gq_ref[...]).astype(BF16)
    q = _dot(qa, wuq_ref[...])
    qn_ref[...] = q[:, :nope].astype(BF16)
    for h in range(N_HEADS):
        lo = nope + h * LANES
        qr_ref[:, h * LANES:(h + 1) * LANES] = _rope_group(q[:, lo:lo + LANES], cos, sin).astype(BF16)
    c_ref[...] = _rms(c1[:, Q_LORA:Q_LORA + KV_LORA], gkv_ref[...])
    kr_ref[...] = _rope_group(c1[:, Q_LORA + KV_LORA:], cos, sin)


def _mla_scores(ql, qr, cb, krb):
    return (_dot_nt(ql, cb) + _dot_nt(qr, krb)) * ((HEAD_DIM + QK_ROPE) ** -0.5)


def _mla_tile(ql, qr, cb, krb, m, l, acc, mask):
    return _softmax_step(_mla_scores(ql, qr, cb, krb), cb, m, l, acc, mask)


def _mla_heads(qn_ref, qr_ref, wuk_ref, wuv_ref, o_ref, attend, tq):
    def cols(ref, lo):
        x = ref[..., lo:lo + LANES]
        return x.reshape(x.shape[-2:])

    pairs_per_group = MLA_HEAD_GROUP // 2
    for grp in range(N_HEADS // MLA_HEAD_GROUP):
        pairs = range(grp * pairs_per_group, (grp + 1) * pairs_per_group)
        qls, qrs = [], []
        for g in pairs:
            qm = _stack_heads(cols(qn_ref, g * LANES).astype(F32))
            qls.append(_dot(qm, wuk_ref[g]).astype(BF16))
            qrs.append(jnp.concatenate([cols(qr_ref, 2 * g * LANES), cols(qr_ref, (2 * g + 1) * LANES)], axis=0))
        o_all = attend([jnp.concatenate(qls, axis=0)], [jnp.concatenate(qrs, axis=0)])[0].astype(BF16)
        for gi, g in enumerate(pairs):
            o_lat = o_all[2 * gi * tq:2 * (gi + 1) * tq]
            out = _dot(o_lat[:tq], wuv_ref[2 * g]) + _dot(o_lat[tq:], wuv_ref[2 * g + 1])
            o_ref[..., g * LANES:(g + 1) * LANES] = out.reshape(o_ref.shape[:-1] + (LANES,)).astype(o_ref.dtype)


def _mla_tiles(qls, qrs, cb, krb, carry, mask):
    return tuple(_mla_tile(ql, qr, cb, krb, *c, mask) for ql, qr, c in zip(qls, qrs, carry))


def _chunk_of(pos):
    return lax.shift_right_arithmetic(pos - N_META, CHUNK.bit_length() - 1)


def _mla_prompt_kernel(qn_ref, qr_ref, cc_ref, kr_ref, wuk_ref, wuv_ref, o_ref, *, seq):
    i = pl.program_id(1)
    tq = KV_TILE
    rows = MLA_HEAD_GROUP * tq
    qs = i * tq
    diag_ks = jnp.minimum(qs, seq - KV_TILE)
    edge_ks = jnp.minimum(qs + KV_TILE, seq - SMALL_TILE)
    q_chunk = _chunk_of(qs + jnp.bitwise_and(_iota((rows, 1), 0), tq - 1))

    def chunk_mask(ks, width, first_new):
        kpos = ks + _iota((1, width), 1)
        return (_chunk_of(kpos) <= q_chunk) & (kpos >= first_new)

    def attend(qls, qrs):
        def tiles(ks, width, carry, mask):
            return _mla_tiles(qls, qrs, _load_tile(cc_ref, ks, width, width, (0,)),
                              _load_tile(kr_ref, ks, width, width, (0,)), carry, mask)

        carry = tuple(_softmax_init(rows, KV_LORA) for _ in qls)
        carry = lax.fori_loop(0, i, lambda j, c: tiles(j * KV_TILE, KV_TILE, c, None), carry)
        carry = tiles(diag_ks, KV_TILE, carry, chunk_mask(diag_ks, KV_TILE, qs))
        carry = tiles(edge_ks, SMALL_TILE, carry, chunk_mask(edge_ks, SMALL_TILE, qs + KV_TILE))
        return [acc / l for m, l, acc in carry]

    _mla_heads(qn_ref, qr_ref, wuk_ref, wuv_ref, o_ref, attend, tq)


def _mla_sample_kernel(qn_ref, qr_ref, cn_ref, krn_ref, cc_ref, krc_ref, wuk_ref, wuv_ref, o_ref, *, past):
    nfull, tail = divmod(past, KV_TILE)
    tq = qn_ref.shape[0]
    rows = MLA_HEAD_GROUP * tq
    col = _iota((rows, SMALL_TILE), 1)
    n_new = cn_ref.shape[0]

    def attend(qls, qrs):
        carry = tuple(_softmax_init(rows, KV_LORA) for _ in qls)
        carry = _mla_tiles(qls, qrs, _load_tile(cn_ref, 0, n_new, SMALL_TILE),
                           _load_tile(krn_ref, 0, n_new, SMALL_TILE), carry, col < n_new)
        if tail:
            carry = _mla_tiles(qls, qrs, _load_tile(cc_ref, nfull * KV_TILE, tail, SMALL_TILE, (0,)),
                               _load_tile(krc_ref, nfull * KV_TILE, tail, SMALL_TILE, (0,)), carry, col < tail)

        def cache_tile(ks, n, carry):
            return _mla_tiles(qls, qrs, _load_tile(cc_ref, ks, n, n, (0,)), _load_tile(krc_ref, ks, n, n, (0,)),
                              carry, None)

        n_wide = nfull // WIDE
        for j in range(n_wide * WIDE, nfull):
            carry = cache_tile(j * KV_TILE, KV_TILE, carry)
        carry = lax.fori_loop(0, n_wide, lambda jw, c: cache_tile(jw * WIDE * KV_TILE, WIDE * KV_TILE, c), carry)
        return [acc / l for m, l, acc in carry]

    _mla_heads(qn_ref, qr_ref, wuk_ref, wuv_ref, o_ref, attend, tq)


def _rope_tables(dims):
    bp, lp, bs, ls, past = dims
    half = QK_ROPE // 2
    pos = jnp.concatenate([jnp.tile(jnp.arange(lp, dtype=jnp.int32), bp),
                           jnp.tile(past + jnp.arange(ls, dtype=jnp.int32), bs)])
    inv = ROPE_THETA ** (-jnp.arange(half, dtype=F32) / half)
    ang = pos.astype(F32)[:, None] * inv
    pad = jnp.zeros((pos.shape[0], LANES - QK_ROPE), F32)
    cos = jnp.concatenate([jnp.cos(ang), jnp.cos(ang), pad], axis=1)
    sin = jnp.concatenate([jnp.sin(ang), jnp.sin(ang), pad], axis=1)
    return cos, sin


def _mla_attention(hb, w, cache_c, cache_kr, dims):
    bp, lp, bs, ls, past = dims
    n = hb.shape[0]
    nope = N_HEADS * HEAD_DIM
    w_dq, g_q, w_uq, w_dkv, g_kv, w_uk, w_uv = w
    w1 = jnp.concatenate([w_dq, w_dkv, jnp.zeros((D_MODEL, LANES - QK_ROPE), F32)], axis=1).astype(BF16)
    c1 = _matmul(hb, w1, name="mla_down_proj")
    wuq = w_uq.reshape(Q_LORA, N_HEADS, HEAD_DIM + QK_ROPE)
    wuq = jnp.concatenate([
        wuq[:, :, :HEAD_DIM].reshape(Q_LORA, nope),
        jnp.pad(wuq[:, :, HEAD_DIM:], ((0, 0), (0, 0), (0, LANES - QK_ROPE))).reshape(Q_LORA, N_HEADS * LANES),
    ], axis=1).astype(BF16)
    cos, sin = _rope_tables(dims)
    tm = _row_tile(n, 256)
    c1w = c1.shape[1]
    row = lambda wd: pl.BlockSpec((tm, wd), lambda i: (i, 0))
    whole = lambda a: pl.BlockSpec(a.shape, lambda i: (0,) * a.ndim)
    gq, gkv = g_q.reshape(1, Q_LORA), g_kv.reshape(1, KV_LORA)
    qn, qr, c, kr = pl.pallas_call(
        _mla_prep_kernel,
        grid=(n // tm,),
        in_specs=[row(c1w), whole(gq), whole(gkv), whole(wuq), row(LANES), row(LANES)],
        out_specs=[row(nope), row(N_HEADS * LANES), row(KV_LORA), row(LANES)],
        out_shape=[jax.ShapeDtypeStruct((n, nope), BF16),
                   jax.ShapeDtypeStruct((n, N_HEADS * LANES), BF16),
                   jax.ShapeDtypeStruct((n, KV_LORA), F32),
                   jax.ShapeDtypeStruct((n, LANES), F32)],
        compiler_params=_cparams(("arbitrary",), VMEM_LIMIT),
        name="mla_prep",
    )(c1, gq, gkv, wuq, cos, sin)

    wuk = jnp.transpose(w_uk, (1, 2, 0)).reshape(N_HEADS // 2, LANES, KV_LORA).astype(BF16)
    wuv = jnp.transpose(w_uv, (1, 0, 2))
    zeros = jnp.zeros_like(wuv)
    even = (jnp.arange(N_HEADS) % 2 == 0)[:, None, None]
    wuv = jnp.concatenate([jnp.where(even, wuv, zeros), jnp.where(even, zeros, wuv)], axis=2).astype(BF16)

    np_ = bp * lp
    seq3 = lambda a, wd: a[:np_].reshape(bp, lp, wd)
    ccb = seq3(c, KV_LORA).astype(BF16)
    krb = seq3(kr, LANES).astype(BF16)
    nq = pl.cdiv(lp, KV_TILE)
    wspec = [pl.BlockSpec(wuk.shape, lambda b, i: (0, 0, 0)), pl.BlockSpec(wuv.shape, lambda b, i: (0, 0, 0))]
    o_p = pl.pallas_call(
        functools.partial(_mla_prompt_kernel, seq=lp),
        grid=(bp, nq),
        in_specs=[pl.BlockSpec((1, KV_TILE, nope), lambda b, i: (b, i, 0)),
                  pl.BlockSpec((1, KV_TILE, N_HEADS * LANES), lambda b, i: (b, i, 0)),
                  pl.BlockSpec((1, lp, KV_LORA), lambda b, i: (b, 0, 0)),
                  pl.BlockSpec((1, lp, LANES), lambda b, i: (b, 0, 0))] + wspec,
        out_specs=pl.BlockSpec((1, KV_TILE, nope), lambda b, i: (b, i, 0)),
        out_shape=jax.ShapeDtypeStruct((bp, lp, nope), BF16),
        compiler_params=_cparams(("arbitrary", "arbitrary"), VMEM_LIMIT),
        name="mla_attn_prompt",
    )(seq3(qn, nope), seq3(qr, N_HEADS * LANES), ccb, krb, wuk, wuv)

    row0 = np_ // ls
    krc = jnp.pad(cache_kr, ((0, 0), (0, 0), (0, LANES - QK_ROPE))).astype(BF16)
    wspec1 = [pl.BlockSpec(wuk.shape, lambda b: (0, 0, 0)), pl.BlockSpec(wuv.shape, lambda b: (0, 0, 0))]
    o_s = pl.pallas_call(
        functools.partial(_mla_sample_kernel, past=past),
        grid=(bs,),
        in_specs=[pl.BlockSpec((ls, nope), lambda b: (row0 + b, 0)),
                  pl.BlockSpec((ls, N_HEADS * LANES), lambda b: (row0 + b, 0)),
                  pl.BlockSpec((ls, KV_LORA), lambda b: (row0 + b, 0)),
                  pl.BlockSpec((ls, LANES), lambda b: (row0 + b, 0)),
                  pl.BlockSpec((1, past, KV_LORA), lambda b: (b, 0, 0)),
                  pl.BlockSpec((1, past, LANES), lambda b: (b, 0, 0))] + wspec1,
        out_specs=pl.BlockSpec((ls, nope), lambda b: (b, 0)),
        out_shape=jax.ShapeDtypeStruct((bs * ls, nope), BF16),
        compiler_params=_cparams(("arbitrary",), VMEM_LIMIT),
        name="mla_attn_sample",
    )(qn, qr, c, kr, cache_c, krc, wuk, wuv)
    o = jnp.concatenate([o_p.reshape(np_, nope), o_s], axis=0)
    return o, c, kr[:, :QK_ROPE]


def _router_kernel(x_ref, w_ref, b_ref, idx_ref, gate_ref):
    xh, xm, _ = _split3(x_ref[...])
    wh, wm, _ = _split3(w_ref[...])
    lg = _dot(xh, wh) + (_dot(xh, wm) + _dot(xm, wh)) + b_ref[...]
    lane = _iota(lg.shape, 1)
    lane_f = lane.astype(F32)
    lg = jnp.where(lane < N_EXPERTS, lg, -jnp.inf)
    vals, idxs = [], []
    for _ in range(TOP_K):
        m = jnp.max(lg, axis=-1, keepdims=True)
        sel = jnp.min(jnp.where(lg == m, lane_f, float(LANES)), axis=-1, keepdims=True)
        vals.append(m)
        idxs.append(sel)
        lg = jnp.where(lane_f == sel, -jnp.inf, lg)
    es = [jnp.exp(v - vals[0]) for v in vals]
    tot = es[0] + es[1] + es[2] + es[3]
    idx = jnp.zeros(lg.shape, F32)
    gate = jnp.zeros(lg.shape, F32)
    for k in range(TOP_K):
        idx = jnp.where(lane == k, idxs[k], idx)
        gate = jnp.where(lane == k, es[k] / tot, gate)
    idx_ref[...] = idx
    gate_ref[...] = gate


def _router(h, w_r, b_r):
    M, D = h.shape
    tm = _row_tile(M, 512)
    w = jnp.pad(w_r, ((0, 0), (0, LANES - N_EXPERTS)))
    b = jnp.pad(b_r, (0, LANES - N_EXPERTS)).reshape(1, LANES)
    row = pl.BlockSpec((tm, LANES), lambda i: (i, 0))
    return pl.pallas_call(
        _router_kernel,
        grid=(M // tm,),
        in_specs=[pl.BlockSpec((tm, D), lambda i: (i, 0)),
                  pl.BlockSpec((D, LANES), lambda i: (0, 0)),
                  pl.BlockSpec((1, LANES), lambda i: (0, 0))],
        out_specs=[row, row],
        out_shape=[jax.ShapeDtypeStruct((M, LANES), F32), jax.ShapeDtypeStruct((M, LANES), F32)],
        compiler_params=_cparams(("arbitrary",)),
        name="moe_router",
    )(h, w, b)


def _slot_kernel(idx_ref, dest_ref, cnt_ref, run_ref, start_ref):
    phase, t = pl.program_id(0), pl.program_id(1)
    idx = idx_ref[...]
    rows = idx.shape[0]
    lane = _iota((rows, LANES), 1).astype(F32)
    onehots = [jnp.where(lane == _lane_pick(idx, k), 1.0, 0.0) for k in range(TOP_K)]

    @pl.when((phase == 0) & (t == 0))
    def _():
        run_ref[...] = jnp.zeros_like(run_ref)

    @pl.when(phase == 0)
    def _():
        tot = onehots[0] + onehots[1] + onehots[2] + onehots[3]
        run_ref[...] += jnp.sum(tot, axis=0, keepdims=True)

    @pl.when((phase == 1) & (t == 0))
    def _():
        counts = run_ref[...]
        cnt_ref[...] = jnp.broadcast_to(counts, cnt_ref.shape)
        padded = jnp.floor((counts + (MOE_TILE - 1)) * (1.0 / MOE_TILE)) * MOE_TILE
        hi, mid, lo = _split3(jnp.broadcast_to(padded, (8, LANES)))
        tri = _tri(LANES, "lt")
        start_ref[...] = (_dot(hi, tri) + _dot(mid, tri) + _dot(lo, tri))[0:1]
        run_ref[...] = jnp.zeros_like(run_ref)

    @pl.when(phase == 1)
    def _():
        tri = _tri(rows, "ge")
        run = run_ref[...]
        out = jnp.zeros((rows, LANES), F32)
        for k, oh in enumerate(onehots):
            cum = _dot(tri, oh.astype(BF16)) + run
            slot = jnp.sum(oh * (start_ref[...] + cum - 1.0), axis=-1, keepdims=True)
            out = jnp.where(lane == k, slot, out)
            run = run + jnp.sum(oh, axis=0, keepdims=True)
        run_ref[...] = run
        dest_ref[...] = out.astype(jnp.int32)


def _slots(idx):
    M = idx.shape[0]
    tm = _row_tile(M, 512)
    return pl.pallas_call(
        _slot_kernel,
        grid=(2, M // tm),
        in_specs=[pl.BlockSpec((tm, LANES), lambda p, t: (t, 0))],
        out_specs=[pl.BlockSpec((tm, LANES), lambda p, t: (t * p, 0)),
                   pl.BlockSpec((8, LANES), lambda p, t: (0, 0))],
        out_shape=[jax.ShapeDtypeStruct((M, LANES), jnp.int32), jax.ShapeDtypeStruct((8, LANES), F32)],
        scratch_shapes=[pltpu.VMEM((1, LANES), F32), pltpu.VMEM((1, LANES), F32)],
        compiler_params=_cparams(("arbitrary", "arbitrary")),
        name="moe_slots",
    )(idx)


def _expert_kernel(te_ref, nt_ref, x_ref, wgu_ref, bgu_ref, wdn_ref, bdn_ref, o_ref, wgu_bf, wdn_bf):
    t = pl.program_id(0)

    @pl.when(t < nt_ref[0])
    def _():
        prev = te_ref[jnp.maximum(t - 1, 0)]

        @pl.when((t == 0) | (te_ref[t] != prev))
        def _():
            wgu_bf[...] = wgu_ref[0, 0].astype(BF16)
            wdn_bf[...] = wdn_ref[0, 0].astype(BF16)

        gu = _dot(x_ref[...].astype(BF16), wgu_bf[...]) + bgu_ref[0, 0]
        gate = jnp.minimum(gu[:, :D_EXPERT], SWIGLU_LIMIT)
        up = jnp.clip(gu[:, D_EXPERT:], -SWIGLU_LIMIT, SWIGLU_LIMIT)
        act = (up + 1.0) * gate * jax.nn.sigmoid(gate * SWIGLU_ALPHA)
        o_ref[...] = _dot(act.astype(BF16), wdn_bf[...]) + bdn_ref[0, 0]

    @pl.when(t >= nt_ref[0])
    def _():
        o_ref[...] = jnp.zeros_like(o_ref)


def _moe(h, layer, w_r, b_r, w_gu, b_gu, w_dn, b_dn):
    n, d = h.shape
    tm = MOE_TILE
    idx, gates = _router(h, w_r, b_r)
    dest, counts = _slots(idx)
    dest = dest[:, :TOP_K]
    counts = counts[0, :N_EXPERTS].astype(jnp.int32)
    ends = jnp.cumsum(((counts + tm - 1) // tm) * tm)
    n_tiles = (n * TOP_K + N_EXPERTS * (tm - 1)) // tm + 1
    slots = n_tiles * tm
    used = (ends[-1] // tm).astype(jnp.int32)
    tile_start = jnp.minimum(jnp.arange(n_tiles, dtype=jnp.int32), used - 1) * tm
    tile_e = jnp.sum((ends[None, :] <= tile_start[:, None]).astype(jnp.int32), axis=1)
    token = jnp.zeros((slots,), jnp.int32).at[dest.reshape(-1)].set(
        jnp.arange(n * TOP_K, dtype=jnp.int32) // TOP_K, mode="promise_in_bounds")
    xs = h.at[token].get(mode="promise_in_bounds")

    tile = lambda t, te, nt: (jnp.minimum(t, nt[0] - 1), 0)
    expert = lambda t, te, nt: (layer, te[t], 0, 0)
    y = pl.pallas_call(
        _expert_kernel,
        grid_spec=pltpu.PrefetchScalarGridSpec(
            num_scalar_prefetch=2,
            grid=(n_tiles,),
            in_specs=[pl.BlockSpec((tm, d), tile),
                      pl.BlockSpec((1, 1, d, 2 * D_EXPERT), expert),
                      pl.BlockSpec((1, 1, 1, 2 * D_EXPERT), expert),
                      pl.BlockSpec((1, 1, D_EXPERT, d), expert),
                      pl.BlockSpec((1, 1, 1, d), expert)],
            out_specs=pl.BlockSpec((tm, d), lambda t, te, nt: (t, 0)),
            scratch_shapes=[pltpu.VMEM((d, 2 * D_EXPERT), BF16), pltpu.VMEM((D_EXPERT, d), BF16)]),
        out_shape=jax.ShapeDtypeStruct((slots, d), F32),
        compiler_params=_cparams(("arbitrary",), VMEM_LIMIT),
        name="moe_experts",
    )(tile_e, used.reshape(1), xs, w_gu, b_gu.reshape(b_gu.shape[:2] + (1, -1)),
      w_dn, b_dn.reshape(b_dn.shape[:2] + (1, -1)))
    ys = [y.at[dest[:, k]].get(mode="promise_in_bounds") for k in range(TOP_K)]
    return ys, gates


def kernel(x_prompt, x_sample, cache_sb_k, cache_sb_v, cache_fox_k, cache_fox_v, cache_fox_logf,
           cache_mla_ckv, cache_mla_krope, meta_tokens, a_w_qkv, a_w_o, b_w_qkv, b_w_f, b_b_f, b_w_o,
           c_w_dq, c_g_q, c_w_uq, c_w_dkv, c_g_kv, c_w_uk, c_w_uv, c_w_o, ln1_g, ln1_b, ln2_g, ln2_b,
           moe_w_router, moe_b_router, moe_w_gate_up, moe_b_gate_up, moe_w_down, moe_b_down):
    bp, sp, d = x_prompt.shape
    bs, ls, _ = x_sample.shape
    lp = N_META + sp
    past = cache_sb_k.shape[2]
    dims = (bp, lp, bs, ls, past)
    np_ = bp * lp
    hd = N_HEADS * HEAD_DIM

    meta = jnp.broadcast_to(meta_tokens[None], (bp, N_META, d))
    h = jnp.concatenate([jnp.concatenate([meta, x_prompt], axis=1).reshape(np_, d),
                         x_sample.reshape(bs * ls, d)], axis=0)
    hb = h.astype(BF16)
    new = {k: [] for k in ("a_k", "a_v", "b_k", "b_v", "b_lf", "c_c", "c_kr")}

    def heads(x, lo):
        return x[:, lo:lo + hd].reshape(-1, N_HEADS, HEAD_DIM)

    sb_k, sb_v, fox_k, fox_v = (c.reshape(c.shape[:3] + (hd,))
                                for c in (cache_sb_k, cache_sb_v, cache_fox_k, cache_fox_v))

    for i in range(DEPTH):
        kind, j = i % 3, i // 3
        if kind == 0:
            qkv = _matmul(hb, a_w_qkv[j].astype(BF16), name="sb_qkv_proj", tn=hd)
            o = _sb_attention(qkv, sb_k, sb_v, j, dims)
            y = _matmul(o, a_w_o[j].astype(BF16), name="sb_out_proj")
            new["a_k"].append(heads(qkv, hd))
            new["a_v"].append(heads(qkv, 2 * hd))
        elif kind == 1:
            qkv = _matmul(hb, b_w_qkv[j].astype(BF16), name="fox_qkv_proj", tn=hd)
            o, lf = _fox_attention(hb, qkv, b_w_f[j], b_b_f[j], fox_k, fox_v, cache_fox_logf[j], j, dims)
            y = _matmul(o, b_w_o[j].astype(BF16), name="fox_out_proj")
            new["b_k"].append(heads(qkv, hd))
            new["b_v"].append(heads(qkv, 2 * hd))
            new["b_lf"].append(lf)
        else:
            o, c, kr = _mla_attention(hb, (c_w_dq[j], c_g_q[j], c_w_uq[j], c_w_dkv[j], c_g_kv[j],
                                           c_w_uk[j], c_w_uv[j]), cache_mla_ckv[j], cache_mla_krope[j], dims)
            y = _matmul(o, c_w_o[j].astype(BF16), name="mla_out_proj")
            new["c_c"].append(c)
            new["c_kr"].append(kr)
        h, hb = _deepnorm_ln(h, [y], None, ln1_g[i], ln1_b[i], name="ln_mixer")
        ys, gates = _moe(h, i, moe_w_router[i], moe_b_router[i], moe_w_gate_up, moe_b_gate_up,
                         moe_w_down, moe_b_down)
        h, hb = _deepnorm_ln(h, ys, gates, ln2_g[i], ln2_b[i], name="ln_moe")

    def split(rows, tail_shape):
        xs = jnp.stack(rows)
        return (xs[:, :np_].reshape((len(rows), bp, lp) + tail_shape),
                xs[:, np_:].reshape((len(rows), bs, ls) + tail_shape))

    outs_p, outs_s = [], []
    for key, tail_shape in (("a_k", (N_HEADS, HEAD_DIM)), ("a_v", (N_HEADS, HEAD_DIM)),
                            ("b_k", (N_HEADS, HEAD_DIM)), ("b_v", (N_HEADS, HEAD_DIM)),
                            ("b_lf", (N_HEADS,)), ("c_c", (KV_LORA,)), ("c_kr", (QK_ROPE,))):
        p, s = split(new[key], tail_shape)
        outs_p.append(p)
        outs_s.append(s)
    y_prompt = h[:np_].reshape(bp, lp, d)[:, N_META:]
    y_sample = h[np_:].reshape(bs, ls, d)
    return (y_prompt, y_sample) + tuple(outs_p) + tuple(outs_s)
```

```python
import functools

import jax
import jax.numpy as jnp
from jax import lax
from jax.experimental import pallas as pl
from jax.experimental.pallas import tpu as pltpu

F32 = jnp.float32
BF16 = jnp.bfloat16

D_MODEL = 1024
N_META = 16
CHUNK = 64
N_HEADS = 16
HEAD_DIM = 64
Q_LORA = 384
KV_LORA = 256
QK_ROPE = 32
ROPE_THETA = 10000.0
N_EXPERTS = 32
TOP_K = 4
D_EXPERT = 1024
SWIGLU_LIMIT = 7.0
SWIGLU_ALPHA = 1.702
DEPTH = 4
DN_ALPHA = (2 * DEPTH) ** 0.25
LN_EPS = 1e-5
RMS_EPS = 1e-6

LANES = 128
KV_TILE = 256
SMALL_TILE = 128
Q_BLOCK = 2 * KV_TILE
WIDE = 4
MLA_HEAD_GROUP = 4
MOE_TILE = 256
NEG = -1e30
VMEM_LIMIT = 56 * 2 ** 20


def _cparams(sem, vmem=None):
    return pltpu.CompilerParams(dimension_semantics=sem, vmem_limit_bytes=vmem)


def _row_tile(n, cap, mult=16):
    t = cap - cap % mult
    while t > mult and n % t:
        t -= mult
    assert n % t == 0, (n, cap)
    return t


def _dot(a, b):
    return jnp.dot(a, b, preferred_element_type=F32)


def _dot_nt(a, b):
    return lax.dot_general(a, b, (((1,), (1,)), ((), ())), preferred_element_type=F32)


def _split3(x):
    hi = x.astype(BF16)
    r = x - hi.astype(F32)
    mid = r.astype(BF16)
    lo = (r - mid.astype(F32)).astype(BF16)
    return hi, mid, lo


def _iota(shape, axis):
    return lax.broadcasted_iota(jnp.int32, shape, axis)


def _tri(n, op):
    r, c = _iota((n, n), 0), _iota((n, n), 1)
    m = {"gt": r > c, "ge": r >= c, "le": r <= c, "lt": r < c}[op]
    return jnp.where(m, 1.0, 0.0).astype(BF16)


def _pad_rows(x, n):
    if x.shape[0] == n:
        return x
    return jnp.concatenate([x, jnp.zeros((n - x.shape[0],) + x.shape[1:], x.dtype)], axis=0)


def _pad_lanes(x, n):
    if x.shape[1] == n:
        return x
    return jnp.concatenate([x, jnp.zeros((x.shape[0], n - x.shape[1]), x.dtype)], axis=1)


def _lane_pick(x, k):
    return jnp.sum(jnp.where(_iota(x.shape, 1) == k, x, 0), axis=-1, keepdims=True)


def _mm_kernel(x_ref, w_ref, o_ref):
    o_ref[...] = _dot(x_ref[...], w_ref[...]).astype(o_ref.dtype)


def _matmul(x, w, *, name, tn=None, out_dtype=F32):
    M, K = x.shape
    N = w.shape[1]
    tn = N if tn is None else tn
    tm = _row_tile(M, 512)
    return pl.pallas_call(
        _mm_kernel,
        grid=(N // tn, M // tm),
        in_specs=[pl.BlockSpec((tm, K), lambda j, i: (i, 0)),
                  pl.BlockSpec((K, tn), lambda j, i: (0, j))],
        out_specs=pl.BlockSpec((tm, tn), lambda j, i: (i, j)),
        out_shape=jax.ShapeDtypeStruct((M, N), out_dtype),
        compiler_params=_cparams(("arbitrary", "arbitrary")),
        name=name,
    )(x, w)


def _layer_norm(x, g, b):
    mu = jnp.mean(x, axis=-1, keepdims=True)
    xc = x - mu
    var = jnp.mean(xc * xc, axis=-1, keepdims=True)
    return xc * lax.rsqrt(var + LN_EPS) * g + b


def _ln_kernel(h_ref, y_ref, g_ref, b_ref, o_ref, ob_ref):
    o = _layer_norm(DN_ALPHA * h_ref[...] + y_ref[...], g_ref[...], b_ref[...])
    o_ref[...] = o
    ob_ref[...] = o.astype(BF16)


def _ln_combine_kernel(h_ref, y0_ref, y1_ref, y2_ref, y3_ref, gate_ref, g_ref, b_ref, o_ref, ob_ref):
    gate = gate_ref[...]
    f = None
    for k, y_ref in enumerate((y0_ref, y1_ref, y2_ref, y3_ref)):
        term = _lane_pick(gate, k) * y_ref[...]
        f = term if f is None else f + term
    o = _layer_norm(DN_ALPHA * h_ref[...] + f, g_ref[...], b_ref[...])
    o_ref[...] = o
    ob_ref[...] = o.astype(BF16)


def _deepnorm_ln(h, ys, gates, g, b, *, name):
    M, D = h.shape
    tm = _row_tile(M, 512)
    row = pl.BlockSpec((tm, D), lambda i: (i, 0))
    vec = pl.BlockSpec((1, D), lambda i: (0, 0))
    extra = [] if gates is None else [pl.BlockSpec((tm, LANES), lambda i: (i, 0))]
    args = list(ys) + ([] if gates is None else [gates])
    return pl.pallas_call(
        _ln_kernel if gates is None else _ln_combine_kernel,
        grid=(M // tm,),
        in_specs=[row] + [row] * len(ys) + extra + [vec, vec],
        out_specs=[row, row],
        out_shape=[jax.ShapeDtypeStruct((M, D), F32), jax.ShapeDtypeStruct((M, D), BF16)],
        compiler_params=_cparams(("arbitrary",)),
        name=name,
    )(h, *args, g.reshape(1, D), b.reshape(1, D))


def _stack_heads(q):
    lane = _iota(q.shape, 1)
    q0 = jnp.where(lane < HEAD_DIM, q, 0.0).astype(BF16)
    q1 = jnp.where(lane >= HEAD_DIM, q, 0.0).astype(BF16)
    return jnp.concatenate([q0, q1], axis=0)


def _unstack_heads(acc, tq):
    lane = _iota((tq, LANES), 1)
    return jnp.where(lane < HEAD_DIM, acc[:tq], acc[tq:])


def _stacked_rows(n_stack, tq, width):
    row = _iota((n_stack * tq, width), 0)
    for _ in range(n_stack - 1):
        row = jnp.where(row >= tq, row - tq, row)
    return row


def _softmax_init(rows, width):
    return (jnp.full((rows, 1), NEG, F32), jnp.zeros((rows, 1), F32), jnp.zeros((rows, width), F32))


def _softmax_step(u, vb, m, l, acc, mask):
    if mask is not None:
        u = jnp.where(mask, u, NEG)
    m_new = jnp.maximum(m, jnp.max(u, axis=-1, keepdims=True))
    p = jnp.exp(u - m_new)
    if mask is not None:
        p = jnp.where(mask, p, 0.0)
    alpha = jnp.exp(m - m_new)
    l = alpha * l + jnp.sum(p, axis=-1, keepdims=True)
    acc = alpha * acc + _dot(p.astype(BF16), vb)
    return m_new, l, acc


def _qkv_specs(seq, row_block0):
    spec = pl.BlockSpec((seq, LANES), lambda b, hp: (row_block0 + b, hp))
    return spec, spec, spec


def _qkv_proj(hb, w_qkv, name):
    w = w_qkv.astype(BF16)
    hd = w.shape[1] // 3
    return tuple(_matmul(hb, w[:, p * hd:(p + 1) * hd], name=name) for p in range(3))


SKIP, MASKED, FULL = "skip", "masked", "full"


def _prompt_blocks(seq, q_block, split):
    nfull, tail = divmod(seq, KV_TILE)
    assert nfull % 2 == 0 and nfull >= 2, seq
    nblk = nfull // 2

    def parts_of(extra):
        return [(0, KV_TILE), (KV_TILE, KV_TILE + extra)] if split else [(0, Q_BLOCK + extra)]

    def modes(first, second):
        return (first, second) if split else (MASKED,)

    def diag_of(qs):
        return [(qs + KV_TILE, KV_TILE, KV_TILE, modes(SKIP, MASKED)), (qs, KV_TILE, KV_TILE, modes(MASKED, FULL))]

    def body(i, carry):
        qs = pl.multiple_of(i * Q_BLOCK, Q_BLOCK)
        q_block(qs, parts_of(0), diag_of(qs), 2 * i)
        return carry

    lax.fori_loop(0, nblk - 1, body, 0)
    qs = (nblk - 1) * Q_BLOCK
    diag = diag_of(qs)
    if tail:
        diag = [(nfull * KV_TILE, tail, SMALL_TILE, modes(SKIP, MASKED))] + diag
    q_block(qs, parts_of(tail), diag, 2 * (nblk - 1))


def _causal_mask(ks, width, q0, rows, strict):
    kpos = ks + _iota((2 * rows, width), 1)
    qpos = q0 + _stacked_rows(2, rows, width)
    return kpos < qpos if strict else kpos <= qpos


def _rows_at(start, n):
    if not isinstance(start, int):
        start = pl.multiple_of(start, 16)
    return pl.ds(start, n)


def _load_tile(ref, ks, n, width, lead=()):
    return _pad_rows(ref[lead + (_rows_at(ks, n), slice(None))].astype(BF16), width)


def _sb_scores(qm, kb):
    s = _dot_nt(qm, kb)
    ls = jnp.minimum(s, 0.0) - jnp.log(1.0 + jnp.exp(-jnp.abs(s)))
    return ls, ls - s


def _sb_tile(qm, kb, vb, r, acc, mask, tri_gt):
    return _sb_accumulate(_sb_scores(qm, kb), vb, r, acc, mask, tri_gt)


def _sb_accumulate(scores, vb, r, acc, mask, tri_gt):
    ls, l = scores
    if mask is not None:
        l = jnp.where(mask, l, 0.0)
    w = tri_gt.shape[0]
    parts = []
    for lo in reversed(range(0, l.shape[1], w)):
        lb = l[:, lo:lo + w]
        lhi = lb.astype(BF16)
        llo = (lb - lhi.astype(F32)).astype(BF16)
        c2 = _dot(jnp.concatenate([lhi, llo], axis=0), tri_gt)
        c = c2[:lb.shape[0]] + c2[lb.shape[0]:]
        parts.append(jnp.exp(ls[:, lo:lo + w] + c + r))
        r = r + jnp.sum(lb, axis=-1, keepdims=True)
    a = parts[0] if len(parts) == 1 else jnp.concatenate(parts[::-1], axis=1)
    if mask is not None:
        a = jnp.where(mask, a, 0.0)
    acc = acc + _dot(a.astype(BF16), vb)
    return r, acc


def _sb_prompt_kernel(q_ref, k_ref, v_ref, o_ref, *, seq):
    tris = {KV_TILE: _tri(KV_TILE, "gt"), SMALL_TILE: _tri(SMALL_TILE, "gt")}
    scale = HEAD_DIM ** -0.5

    def q_block(qs, halves, diag, n_left):
        qms = [_stack_heads(q_ref[_rows_at(qs + off, rows), :] * scale) for off, rows in halves]
        carry = [(jnp.zeros((2 * rows, 1), F32), jnp.zeros((2 * rows, LANES), F32)) for _, rows in halves]
        for ks, n, width, modes in diag:
            kb, vb = _load_tile(k_ref, ks, n, width), _load_tile(v_ref, ks, n, width)
            for h, ((off, rows), mode) in enumerate(zip(halves, modes)):
                if mode != SKIP:
                    mask = _causal_mask(ks, width, qs + off, rows, True) if mode == MASKED else None
                    carry[h] = _sb_tile(qms[h], kb, vb, *carry[h], mask, tris[width])

        def merge(xs):
            return jnp.concatenate([x[:rows] for x, (_, rows) in zip(xs, halves)]
                                   + [x[rows:] for x, (_, rows) in zip(xs, halves)], axis=0)

        qm = merge(qms)
        carry = (merge([c[0] for c in carry]), merge([c[1] for c in carry]))

        def body(jj, carry):
            ks = (n_left - 1 - jj) * KV_TILE
            kb, vb = _load_tile(k_ref, ks, KV_TILE, KV_TILE), _load_tile(v_ref, ks, KV_TILE, KV_TILE)
            return _sb_tile(qm, kb, vb, *carry, None, tris[KV_TILE])

        r, acc = lax.fori_loop(0, n_left, body, carry)
        tq = sum(rows for _, rows in halves)
        o_ref[_rows_at(qs, tq), :] = _unstack_heads(acc, tq).astype(o_ref.dtype)

    _prompt_blocks(seq, q_block, split=True)


def _sb_sample_kernel(q_ref, kn_ref, vn_ref, kc_ref, vc_ref, o_ref, *, past):
    nfull, tail = divmod(past, KV_TILE)
    tq = q_ref.shape[0]
    tri_full, tri_small = _tri(KV_TILE, "gt"), _tri(SMALL_TILE, "gt")
    qm = _stack_heads(q_ref[...] * (HEAD_DIM ** -0.5))
    row, col = _stacked_rows(2, tq, SMALL_TILE), _iota((2 * tq, SMALL_TILE), 1)
    r = jnp.zeros((2 * tq, 1), F32)
    acc = jnp.zeros((2 * tq, LANES), F32)
    n_new = kn_ref.shape[0]
    r, acc = _sb_tile(qm, _load_tile(kn_ref, 0, n_new, SMALL_TILE), _load_tile(vn_ref, 0, n_new, SMALL_TILE),
                      r, acc, col < row, tri_small)
    if tail:
        r, acc = _sb_tile(qm, _load_tile(kc_ref, nfull * KV_TILE, tail, SMALL_TILE, (0,)),
                          _load_tile(vc_ref, nfull * KV_TILE, tail, SMALL_TILE, (0,)),
                          r, acc, col < tail, tri_small)

    def cache_tile(ks, n, carry):
        return _sb_tile(qm, _load_tile(kc_ref, ks, n, n, (0,)), _load_tile(vc_ref, ks, n, n, (0,)),
                        carry[0], carry[1], None, tri_full)

    n_wide = nfull // WIDE
    carry = (r, acc)
    for j in reversed(range(n_wide * WIDE, nfull)):
        carry = cache_tile(j * KV_TILE, KV_TILE, carry)
    r, acc = lax.fori_loop(0, n_wide, lambda jj, c: cache_tile((n_wide - 1 - jj) * WIDE * KV_TILE,
                                                                 WIDE * KV_TILE, c), carry)
    o_ref[...] = _unstack_heads(acc, tq).astype(o_ref.dtype)


def _sb_attention(qkv, cache_k, cache_v, layer, dims):
    bp, lp, bs, ls, past = dims
    hp = N_HEADS // 2
    o_p = pl.pallas_call(
        functools.partial(_sb_prompt_kernel, seq=lp),
        grid=(bp, hp),
        in_specs=list(_qkv_specs(lp, 0)),
        out_specs=pl.BlockSpec((lp, LANES), lambda b, h: (b, h)),
        out_shape=jax.ShapeDtypeStruct((bp * lp, N_HEADS * HEAD_DIM), BF16),
        compiler_params=_cparams(("arbitrary", "arbitrary"), VMEM_LIMIT),
        name="sb_attn_prompt",
    )(*qkv)
    cache_spec = pl.BlockSpec((None, 1, past, LANES), lambda b, h: (layer, b, 0, h))
    o_s = pl.pallas_call(
        functools.partial(_sb_sample_kernel, past=past),
        grid=(bs, hp),
        in_specs=list(_qkv_specs(ls, bp * lp // ls)) + [cache_spec, cache_spec],
        out_specs=pl.BlockSpec((ls, LANES), lambda b, h: (b, h)),
        out_shape=jax.ShapeDtypeStruct((bs * ls, N_HEADS * HEAD_DIM), BF16),
        compiler_params=_cparams(("arbitrary", "arbitrary"), VMEM_LIMIT),
        name="sb_attn_sample",
    )(*qkv, cache_k, cache_v)
    return jnp.concatenate([o_p, o_s], axis=0)


def _log_sigmoid(x):
    return jnp.minimum(x, 0.0) - jnp.log1p(jnp.exp(-jnp.abs(x)))


def _cumsum_rows(x, tri_ge):
    hi, mid, lo = _split3(x)
    return _dot(tri_ge, hi) + _dot(tri_ge, mid) + _dot(tri_ge, lo)


def _cumsum_lanes(x, tri_le):
    hi, mid, lo = _split3(x)
    return _dot(hi, tri_le) + _dot(mid, tri_le) + _dot(lo, tri_le)


def _gate_chunk(hc, valid, wf_ref, wft_ref, bf_row_ref, bf_col_ref):
    n = hc.shape[0]
    g = _dot(hc, wf_ref[...]) + bf_row_ref[...]
    gt = _dot_nt(wft_ref[...], hc) + bf_col_ref[...]
    lf, lft = _log_sigmoid(g), _log_sigmoid(gt)
    if valid < n:
        lf = jnp.where(_iota(lf.shape, 0) < valid, lf, 0.0)
        lft = jnp.where(_iota(lft.shape, 1) < valid, lft, 0.0)
    return lf, lft


def _fox_gate_prompt_kernel(h_ref, wf_ref, wft_ref, bfr_ref, bfc_ref, lf_ref, cum_ref, cumt_ref, *, seq):
    nfull, tail = divmod(seq, KV_TILE)
    carry = jnp.zeros((1, LANES), F32)
    carry_t = jnp.zeros((N_HEADS, 1), F32)
    for j in range(nfull + (1 if tail else 0)):
        n = KV_TILE if j < nfull else SMALL_TILE
        valid = KV_TILE if j < nfull else tail
        hc = _pad_rows(h_ref[j * KV_TILE:j * KV_TILE + valid, :], n)
        lf, lft = _gate_chunk(hc, valid, wf_ref, wft_ref, bfr_ref, bfc_ref)
        cum = _cumsum_rows(lf, _tri(n, "ge")) + carry
        cum_t = _cumsum_lanes(lft, _tri(n, "le")) + carry_t
        lf_ref[j * KV_TILE:j * KV_TILE + valid, :] = lf[:valid]
        cum_ref[j * KV_TILE:j * KV_TILE + valid, :] = cum[:valid]
        cumt_ref[0, j] = _pad_lanes(cum_t, KV_TILE)
        carry = cum[valid - 1:valid, :]
        carry_t = cum_t[:, valid - 1:valid]


def _fox_gate_sample_kernel(h_ref, plft_ref, wf_ref, wft_ref, bfr_ref, bfc_ref,
                            lf_ref, cum_ref, cumt_past_ref, cumt_new_ref):
    ntiles = plft_ref.shape[2] // KV_TILE
    carry_t = jnp.zeros((N_HEADS, 1), F32)
    tri_le = _tri(KV_TILE, "le")
    for j in range(ntiles):
        cum_t = _cumsum_lanes(plft_ref[0, :, j * KV_TILE:(j + 1) * KV_TILE], tri_le) + carry_t
        cumt_past_ref[0, j] = cum_t
        carry_t = cum_t[:, KV_TILE - 1:KV_TILE]
    ls = h_ref.shape[0]
    hc = _pad_rows(h_ref[...], SMALL_TILE)
    lf, lft = _gate_chunk(hc, ls, wf_ref, wft_ref, bfr_ref, bfc_ref)
    eye = _iota((N_HEADS, LANES), 0) == _iota((N_HEADS, LANES), 1)
    carry = jnp.sum(jnp.where(eye, carry_t, 0.0), axis=0, keepdims=True)
    cum = _cumsum_rows(lf, _tri(SMALL_TILE, "ge")) + carry
    cum_t = _cumsum_lanes(lft, _tri(SMALL_TILE, "le")) + carry_t
    lf_ref[...] = lf[:ls]
    cum_ref[...] = cum[:ls]
    cumt_new_ref[0] = cum_t


def _fx_tile(qm, kb, vb, ck, cq, m, l, acc, mask):
    return _softmax_step(_fx_scores(qm, kb, ck, cq), vb, m, l, acc, mask)


def _fx_scores(qm, kb, ck, cq):
    s = _dot_nt(qm, kb)
    tq = s.shape[0] // 2
    return jnp.concatenate([(s[:tq] + cq[:tq]) - ck[0], (s[tq:] + cq[tq:]) - ck[1]], axis=0)


def _head_rows(ref, lead, head0, width):
    return tuple(ref[lead + (pl.ds(head0 + hh, 1), slice(None))][:, :width] for hh in range(2))


def _fox_prompt_kernel(q_ref, k_ref, v_ref, cum_ref, cumt_ref, o_ref, *, seq):
    head0 = 2 * pl.program_id(1)
    scale = HEAD_DIM ** -0.5

    def ck_rows(j, width):
        return _head_rows(cumt_ref, (0, j), head0, width)

    def q_block(qs, halves, diag, n_left):
        qms, cqs, carry = [], [], []
        for off, rows in halves:
            qms.append(_stack_heads(q_ref[_rows_at(qs + off, rows), :] * scale))
            cum_q = cum_ref[_rows_at(qs + off, rows), :]
            cqs.append(jnp.concatenate([_lane_pick(cum_q, head0), _lane_pick(cum_q, head0 + 1)], axis=0))
            carry.append(_softmax_init(2 * rows, LANES))
        for ks, n, width, modes in diag:
            kb, vb = _load_tile(k_ref, ks, n, width), _load_tile(v_ref, ks, n, width)
            ck = ck_rows(ks // KV_TILE, width)
            for h, ((off, rows), mode) in enumerate(zip(halves, modes)):
                if mode != SKIP:
                    mask = _causal_mask(ks, width, qs + off, rows, False) if mode == MASKED else None
                    carry[h] = _fx_tile(qms[h], kb, vb, ck, cqs[h], *carry[h], mask)

        def body(j, carry):
            ks = j * KV_TILE
            kb, vb = _load_tile(k_ref, ks, KV_TILE, KV_TILE), _load_tile(v_ref, ks, KV_TILE, KV_TILE)
            ck = ck_rows(j, KV_TILE)
            return tuple(_fx_tile(qm, kb, vb, ck, cq, *c, None) for qm, cq, c in zip(qms, cqs, carry))

        carry = lax.fori_loop(0, n_left, body, tuple(carry))
        for (off, rows), (m, l, acc) in zip(halves, carry):
            o_ref[_rows_at(qs + off, rows), :] = _unstack_heads(acc / l, rows).astype(o_ref.dtype)

    _prompt_blocks(seq, q_block, split=True)


def _fox_sample_kernel(q_ref, kn_ref, vn_ref, kc_ref, vc_ref, cum_ref, cumt_past_ref, cumt_new_ref,
                       o_ref, *, past):
    nfull, tail = divmod(past, KV_TILE)
    tq = q_ref.shape[0]
    head0 = 2 * pl.program_id(1)
    qm = _stack_heads(q_ref[...] * (HEAD_DIM ** -0.5))
    cum_q = cum_ref[...]
    cq = jnp.concatenate([_lane_pick(cum_q, head0), _lane_pick(cum_q, head0 + 1)], axis=0)
    row, col = _stacked_rows(2, tq, SMALL_TILE), _iota((2 * tq, SMALL_TILE), 1)
    n_new = kn_ref.shape[0]
    carry = _softmax_init(2 * tq, LANES)
    carry = _fx_tile(qm, _load_tile(kn_ref, 0, n_new, SMALL_TILE), _load_tile(vn_ref, 0, n_new, SMALL_TILE),
                     _head_rows(cumt_new_ref, (0,), head0, SMALL_TILE), cq, *carry, col <= row)
    if tail:
        carry = _fx_tile(qm, _load_tile(kc_ref, nfull * KV_TILE, tail, SMALL_TILE, (0,)),
                         _load_tile(vc_ref, nfull * KV_TILE, tail, SMALL_TILE, (0,)),
                         _head_rows(cumt_past_ref, (0, nfull), head0, SMALL_TILE), cq, *carry, col < tail)

    def cache_tile(j, n_tiles, carry):
        ks, n = j * KV_TILE, n_tiles * KV_TILE
        rows = [_head_rows(cumt_past_ref, (0, j + t), head0, KV_TILE) for t in range(n_tiles)]
        ck = tuple(rows[0][hh] if n_tiles == 1 else jnp.concatenate([r[hh] for r in rows], axis=1)
                   for hh in range(2))
        return _fx_tile(qm, _load_tile(kc_ref, ks, n, n, (0,)), _load_tile(vc_ref, ks, n, n, (0,)),
                        ck, cq, *carry, None)

    n_wide = nfull // WIDE
    for j in range(n_wide * WIDE, nfull):
        carry = cache_tile(j, 1, carry)
    m, l, acc = lax.fori_loop(0, n_wide, lambda jw, c: cache_tile(jw * WIDE, WIDE, c), carry)
    o_ref[...] = _unstack_heads(acc / l, tq).astype(o_ref.dtype)


def _fox_attention(hb, qkv, w_f, b_f, cache_k, cache_v, cache_lf, layer, dims):
    bp, lp, bs, ls, past = dims
    hp = N_HEADS // 2
    ntp = pl.cdiv(lp, KV_TILE)
    wf = jnp.pad(w_f, ((0, 0), (0, LANES - N_HEADS))).astype(BF16)
    wft = w_f.T.astype(BF16)
    bfr = jnp.pad(b_f, (0, LANES - N_HEADS)).reshape(1, LANES)
    bfc = b_f.reshape(N_HEADS, 1)
    whole = lambda a: pl.BlockSpec(a.shape, lambda b: (0,) * a.ndim)
    wspecs = [whole(wf), whole(wft), whole(bfr), whole(bfc)]
    lf_p, cum_p, cumt_p = pl.pallas_call(
        functools.partial(_fox_gate_prompt_kernel, seq=lp),
        grid=(bp,),
        in_specs=[pl.BlockSpec((lp, D_MODEL), lambda b: (b, 0))] + wspecs,
        out_specs=[pl.BlockSpec((lp, LANES), lambda b: (b, 0)),
                   pl.BlockSpec((lp, LANES), lambda b: (b, 0)),
                   pl.BlockSpec((1, ntp, N_HEADS, KV_TILE), lambda b: (b, 0, 0, 0))],
        out_shape=[jax.ShapeDtypeStruct((bp * lp, LANES), F32),
                   jax.ShapeDtypeStruct((bp * lp, LANES), F32),
                   jax.ShapeDtypeStruct((bp, ntp, N_HEADS, KV_TILE), F32)],
        compiler_params=_cparams(("arbitrary",), VMEM_LIMIT),
        name="fox_gate_prompt",
    )(hb, wf, wft, bfr, bfc)
    nts = pl.cdiv(past, KV_TILE)
    plft = jnp.pad(jnp.swapaxes(cache_lf, 1, 2), ((0, 0), (0, 0), (0, nts * KV_TILE - past)))
    row0 = bp * lp // ls
    lf_s, cum_s, cumt_past, cumt_new = pl.pallas_call(
        _fox_gate_sample_kernel,
        grid=(bs,),
        in_specs=[pl.BlockSpec((ls, D_MODEL), lambda b: (row0 + b, 0)),
                  pl.BlockSpec((1, N_HEADS, nts * KV_TILE), lambda b: (b, 0, 0))] + wspecs,
        out_specs=[pl.BlockSpec((ls, LANES), lambda b: (b, 0)),
                   pl.BlockSpec((ls, LANES), lambda b: (b, 0)),
                   pl.BlockSpec((1, nts, N_HEADS, KV_TILE), lambda b: (b, 0, 0, 0)),
                   pl.BlockSpec((1, N_HEADS, SMALL_TILE), lambda b: (b, 0, 0))],
        out_shape=[jax.ShapeDtypeStruct((bs * ls, LANES), F32),
                   jax.ShapeDtypeStruct((bs * ls, LANES), F32),
                   jax.ShapeDtypeStruct((bs, nts, N_HEADS, KV_TILE), F32),
                   jax.ShapeDtypeStruct((bs, N_HEADS, SMALL_TILE), F32)],
        compiler_params=_cparams(("arbitrary",), VMEM_LIMIT),
        name="fox_gate_sample",
    )(hb, plft, wf, wft, bfr, bfc)

    o_p = pl.pallas_call(
        functools.partial(_fox_prompt_kernel, seq=lp),
        grid=(bp, hp),
        in_specs=list(_qkv_specs(lp, 0)) + [
            pl.BlockSpec((lp, LANES), lambda b, h: (b, 0)),
            pl.BlockSpec((1, ntp, N_HEADS, KV_TILE), lambda b, h: (b, 0, 0, 0))],
        out_specs=pl.BlockSpec((lp, LANES), lambda b, h: (b, h)),
        out_shape=jax.ShapeDtypeStruct((bp * lp, N_HEADS * HEAD_DIM), BF16),
        compiler_params=_cparams(("arbitrary", "arbitrary"), VMEM_LIMIT),
        name="fox_attn_prompt",
    )(*qkv, cum_p, cumt_p)
    cache_spec = pl.BlockSpec((None, 1, past, LANES), lambda b, h: (layer, b, 0, h))
    o_s = pl.pallas_call(
        functools.partial(_fox_sample_kernel, past=past),
        grid=(bs, hp),
        in_specs=list(_qkv_specs(ls, row0)) + [
            cache_spec, cache_spec,
            pl.BlockSpec((ls, LANES), lambda b, h: (b, 0)),
            pl.BlockSpec((1, nts, N_HEADS, KV_TILE), lambda b, h: (b, 0, 0, 0)),
            pl.BlockSpec((1, N_HEADS, SMALL_TILE), lambda b, h: (b, 0, 0))],
        out_specs=pl.BlockSpec((ls, LANES), lambda b, h: (b, h)),
        out_shape=jax.ShapeDtypeStruct((bs * ls, N_HEADS * HEAD_DIM), BF16),
        compiler_params=_cparams(("arbitrary", "arbitrary"), VMEM_LIMIT),
        name="fox_attn_sample",
    )(*qkv, cache_k, cache_v, cum_s, cumt_past, cumt_new)
    o = jnp.concatenate([o_p, o_s], axis=0)
    lf = jnp.concatenate([lf_p, lf_s], axis=0)[:, :N_HEADS]
    return o, lf


def _rms(x, g):
    return x * lax.rsqrt(jnp.mean(x * x, axis=-1, keepdims=True) + RMS_EPS) * g


def _rope_group(x, cos, sin):
    half = QK_ROPE // 2
    lane = _iota(x.shape, 1)
    from_hi = pltpu.roll(x, LANES - half, axis=1)
    from_lo = pltpu.roll(x, half, axis=1)
    rot = jnp.where(lane < half, -from_hi, from_lo)
    return x * cos + rot * sin


def _mla_prep_kernel(c1_ref, gq_ref, gkv_ref, wuq_ref, cos_ref, sin_ref,
                     qn_ref, qr_ref, c_ref, kr_ref):
    nope = N_HEADS * HEAD_DIM
    c1 = c1_ref[...]
    cos, sin = cos_ref[...], sin_ref[...]
    qa = _rms(c1[:, :Q_LORA], gq_ref[...]).astype(BF16)
    q = _dot(qa, wuq_ref[...])
    qn_ref[...] = q[:, :nope].astype(BF16)
    for h in range(N_HEADS):
        lo = nope + h * LANES
        qr_ref[:, h * LANES:(h + 1) * LANES] = _rope_group(q[:, lo:lo + LANES], cos, sin).astype(BF16)
    c_ref[...] = _rms(c1[:, Q_LORA:Q_LORA + KV_LORA], gkv_ref[...])
    kr_ref[...] = _rope_group(c1[:, Q_LORA + KV_LORA:], cos, sin)


def _mla_scores(ql, qr, cb, krb):
    return (_dot_nt(ql, cb) + _dot_nt(qr, krb)) * ((HEAD_DIM + QK_ROPE) ** -0.5)


def _mla_tile(ql, qr, cb, krb, m, l, acc, mask):
    return _softmax_step(_mla_scores(ql, qr, cb, krb), cb, m, l, acc, mask)


def _mla_heads(qn_ref, qr_ref, wuk_ref, wuv_ref, o_ref, attend, tq):
    def cols(ref, lo):
        x = ref[..., lo:lo + LANES]
        return x.reshape(x.shape[-2:])

    pairs_per_group = MLA_HEAD_GROUP // 2
    for grp in range(N_HEADS // MLA_HEAD_GROUP):
        pairs = range(grp * pairs_per_group, (grp + 1) * pairs_per_group)
        qls, qrs = [], []
        for g in pairs:
            qm = _stack_heads(cols(qn_ref, g * LANES).astype(F32))
            qls.append(_dot(qm, wuk_ref[g]).astype(BF16))
            qrs.append(jnp.concatenate([cols(qr_ref, 2 * g * LANES), cols(qr_ref, (2 * g + 1) * LANES)], axis=0))
        o_all = attend([jnp.concatenate(qls, axis=0)], [jnp.concatenate(qrs, axis=0)])[0].astype(BF16)
        for gi, g in enumerate(pairs):
            o_lat = o_all[2 * gi * tq:2 * (gi + 1) * tq]
            out = _dot(o_lat[:tq], wuv_ref[2 * g]) + _dot(o_lat[tq:], wuv_ref[2 * g + 1])
            o_ref[..., g * LANES:(g + 1) * LANES] = out.reshape(o_ref.shape[:-1] + (LANES,)).astype(o_ref.dtype)


def _mla_tiles(qls, qrs, cb, krb, carry, mask):
    return tuple(_mla_tile(ql, qr, cb, krb, *c, mask) for ql, qr, c in zip(qls, qrs, carry))


def _chunk_of(pos):
    return lax.shift_right_arithmetic(pos - N_META, CHUNK.bit_length() - 1)


def _mla_prompt_kernel(qn_ref, qr_ref, cc_ref, kr_ref, wuk_ref, wuv_ref, o_ref, *, seq):
    i = pl.program_id(1)
    tq = KV_TILE
    rows = MLA_HEAD_GROUP * tq
    qs = i * tq
    diag_ks = jnp.minimum(qs, seq - KV_TILE)
    edge_ks = jnp.minimum(qs + KV_TILE, seq - SMALL_TILE)
    q_chunk = _chunk_of(qs + jnp.bitwise_and(_iota((rows, 1), 0), tq - 1))

    def chunk_mask(ks, width, first_new):
        kpos = ks + _iota((1, width), 1)
        return (_chunk_of(kpos) <= q_chunk) & (kpos >= first_new)

    def attend(qls, qrs):
        def tiles(ks, width, carry, mask):
            return _mla_tiles(qls, qrs, _load_tile(cc_ref, ks, width, width, (0,)),
                              _load_tile(kr_ref, ks, width, width, (0,)), carry, mask)

        carry = tuple(_softmax_init(rows, KV_LORA) for _ in qls)
        carry = lax.fori_loop(0, i, lambda j, c: tiles(j * KV_TILE, KV_TILE, c, None), carry)
        carry = tiles(diag_ks, KV_TILE, carry, chunk_mask(diag_ks, KV_TILE, qs))
        carry = tiles(edge_ks, SMALL_TILE, carry, chunk_mask(edge_ks, SMALL_TILE, qs + KV_TILE))
        return [acc / l for m, l, acc in carry]

    _mla_heads(qn_ref, qr_ref, wuk_ref, wuv_ref, o_ref, attend, tq)


def _mla_sample_kernel(qn_ref, qr_ref, cn_ref, krn_ref, cc_ref, krc_ref, wuk_ref, wuv_ref, o_ref, *, past):
    nfull, tail = divmod(past, KV_TILE)
    tq = qn_ref.shape[0]
    rows = MLA_HEAD_GROUP * tq
    col = _iota((rows, SMALL_TILE), 1)
    n_new = cn_ref.shape[0]

    def attend(qls, qrs):
        carry = tuple(_softmax_init(rows, KV_LORA) for _ in qls)
        carry = _mla_tiles(qls, qrs, _load_tile(cn_ref, 0, n_new, SMALL_TILE),
                           _load_tile(krn_ref, 0, n_new, SMALL_TILE), carry, col < n_new)
        if tail:
            carry = _mla_tiles(qls, qrs, _load_tile(cc_ref, nfull * KV_TILE, tail, SMALL_TILE, (0,)),
                               _load_tile(krc_ref, nfull * KV_TILE, tail, SMALL_TILE, (0,)), carry, col < tail)

        def cache_tile(ks, n, carry):
            return _mla_tiles(qls, qrs, _load_tile(cc_ref, ks, n, n, (0,)), _load_tile(krc_ref, ks, n, n, (0,)),
                              carry, None)

        n_wide = nfull // WIDE
        for j in range(n_wide * WIDE, nfull):
            carry = cache_tile(j * KV_TILE, KV_TILE, carry)
        carry = lax.fori_loop(0, n_wide, lambda jw, c: cache_tile(jw * WIDE * KV_TILE, WIDE * KV_TILE, c), carry)
        return [acc / l for m, l, acc in carry]

    _mla_heads(qn_ref, qr_ref, wuk_ref, wuv_ref, o_ref, attend, tq)


def _rope_tables(dims):
    bp, lp, bs, ls, past = dims
    half = QK_ROPE // 2
    pos = jnp.concatenate([jnp.tile(jnp.arange(lp, dtype=jnp.int32), bp),
                           jnp.tile(past + jnp.arange(ls, dtype=jnp.int32), bs)])
    inv = ROPE_THETA ** (-jnp.arange(half, dtype=F32) / half)
    ang = pos.astype(F32)[:, None] * inv
    pad = jnp.zeros((pos.shape[0], LANES - QK_ROPE), F32)
    cos = jnp.concatenate([jnp.cos(ang), jnp.cos(ang), pad], axis=1)
    sin = jnp.concatenate([jnp.sin(ang), jnp.sin(ang), pad], axis=1)
    return cos, sin


def _mla_attention(hb, w, cache_c, cache_kr, dims):
    bp, lp, bs, ls, past = dims
    n = hb.shape[0]
    nope = N_HEADS * HEAD_DIM
    w_dq, g_q, w_uq, w_dkv, g_kv, w_uk, w_uv = w
    w1 = jnp.concatenate([w_dq, w_dkv, jnp.zeros((D_MODEL, LANES - QK_ROPE), F32)], axis=1).astype(BF16)
    c1 = _matmul(hb, w1, name="mla_down_proj")
    wuq = w_uq.reshape(Q_LORA, N_HEADS, HEAD_DIM + QK_ROPE)
    wuq = jnp.concatenate([
        wuq[:, :, :HEAD_DIM].reshape(Q_LORA, nope),
        jnp.pad(wuq[:, :, HEAD_DIM:], ((0, 0), (0, 0), (0, LANES - QK_ROPE))).reshape(Q_LORA, N_HEADS * LANES),
    ], axis=1).astype(BF16)
    cos, sin = _rope_tables(dims)
    tm = _row_tile(n, 256)
    c1w = c1.shape[1]
    row = lambda wd: pl.BlockSpec((tm, wd), lambda i: (i, 0))
    whole = lambda a: pl.BlockSpec(a.shape, lambda i: (0,) * a.ndim)
    gq, gkv = g_q.reshape(1, Q_LORA), g_kv.reshape(1, KV_LORA)
    qn, qr, c, kr = pl.pallas_call(
        _mla_prep_kernel,
        grid=(n // tm,),
        in_specs=[row(c1w), whole(gq), whole(gkv), whole(wuq), row(LANES), row(LANES)],
        out_specs=[row(nope), row(N_HEADS * LANES), row(KV_LORA), row(LANES)],
        out_shape=[jax.ShapeDtypeStruct((n, nope), BF16),
                   jax.ShapeDtypeStruct((n, N_HEADS * LANES), BF16),
                   jax.ShapeDtypeStruct((n, KV_LORA), F32),
                   jax.ShapeDtypeStruct((n, LANES), F32)],
        compiler_params=_cparams(("arbitrary",), VMEM_LIMIT),
        name="mla_prep",
    )(c1, gq, gkv, wuq, cos, sin)

    wuk = jnp.transpose(w_uk, (1, 2, 0)).reshape(N_HEADS // 2, LANES, KV_LORA).astype(BF16)
    wuv = jnp.transpose(w_uv, (1, 0, 2))
    zeros = jnp.zeros_like(wuv)
    even = (jnp.arange(N_HEADS) % 2 == 0)[:, None, None]
    wuv = jnp.concatenate([jnp.where(even, wuv, zeros), jnp.where(even, zeros, wuv)], axis=2).astype(BF16)

    np_ = bp * lp
    seq3 = lambda a, wd: a[:np_].reshape(bp, lp, wd)
    ccb = seq3(c, KV_LORA).astype(BF16)
    krb = seq3(kr, LANES).astype(BF16)
    nq = pl.cdiv(lp, KV_TILE)
    wspec = [pl.BlockSpec(wuk.shape, lambda b, i: (0, 0, 0)), pl.BlockSpec(wuv.shape, lambda b, i: (0, 0, 0))]
    o_p = pl.pallas_call(
        functools.partial(_mla_prompt_kernel, seq=lp),
        grid=(bp, nq),
        in_specs=[pl.BlockSpec((1, KV_TILE, nope), lambda b, i: (b, i, 0)),
                  pl.BlockSpec((1, KV_TILE, N_HEADS * LANES), lambda b, i: (b, i, 0)),
                  pl.BlockSpec((1, lp, KV_LORA), lambda b, i: (b, 0, 0)),
                  pl.BlockSpec((1, lp, LANES), lambda b, i: (b, 0, 0))] + wspec,
        out_specs=pl.BlockSpec((1, KV_TILE, nope), lambda b, i: (b, i, 0)),
        out_shape=jax.ShapeDtypeStruct((bp, lp, nope), BF16),
        compiler_params=_cparams(("arbitrary", "arbitrary"), VMEM_LIMIT),
        name="mla_attn_prompt",
    )(seq3(qn, nope), seq3(qr, N_HEADS * LANES), ccb, krb, wuk, wuv)

    row0 = np_ // ls
    krc = jnp.pad(cache_kr, ((0, 0), (0, 0), (0, LANES - QK_ROPE))).astype(BF16)
    wspec1 = [pl.BlockSpec(wuk.shape, lambda b: (0, 0, 0)), pl.BlockSpec(wuv.shape, lambda b: (0, 0, 0))]
    o_s = pl.pallas_call(
        functools.partial(_mla_sample_kernel, past=past),
        grid=(bs,),
        in_specs=[pl.BlockSpec((ls, nope), lambda b: (row0 + b, 0)),
                  pl.BlockSpec((ls, N_HEADS * LANES), lambda b: (row0 + b, 0)),
                  pl.BlockSpec((ls, KV_LORA), lambda b: (row0 + b, 0)),
                  pl.BlockSpec((ls, LANES), lambda b: (row0 + b, 0)),
                  pl.BlockSpec((1, past, KV_LORA), lambda b: (b, 0, 0)),
                  pl.BlockSpec((1, past, LANES), lambda b: (b, 0, 0))] + wspec1,
        out_specs=pl.BlockSpec((ls, nope), lambda b: (b, 0)),
        out_shape=jax.ShapeDtypeStruct((bs * ls, nope), BF16),
        compiler_params=_cparams(("arbitrary",), VMEM_LIMIT),
        name="mla_attn_sample",
    )(qn, qr, c, kr, cache_c, krc, wuk, wuv)
    o = jnp.concatenate([o_p.reshape(np_, nope), o_s], axis=0)
    return o, c, kr[:, :QK_ROPE]


def _router_kernel(x_ref, w_ref, b_ref, idx_ref, gate_ref):
    xh, xm, _ = _split3(x_ref[...])
    wh, wm, _ = _split3(w_ref[...])
    lg = _dot(xh, wh) + (_dot(xh, wm) + _dot(xm, wh)) + b_ref[...]
    lane = _iota(lg.shape, 1)
    lane_f = lane.astype(F32)
    lg = jnp.where(lane < N_EXPERTS, lg, -jnp.inf)
    vals, idxs = [], []
    for _ in range(TOP_K):
        m = jnp.max(lg, axis=-1, keepdims=True)
        sel = jnp.min(jnp.where(lg == m, lane_f, float(LANES)), axis=-1, keepdims=True)
        vals.append(m)
        idxs.append(sel)
        lg = jnp.where(lane_f == sel, -jnp.inf, lg)
    es = [jnp.exp(v - vals[0]) for v in vals]
    tot = es[0] + es[1] + es[2] + es[3]
    idx = jnp.zeros(lg.shape, F32)
    gate = jnp.zeros(lg.shape, F32)
    for k in range(TOP_K):
        idx = jnp.where(lane == k, idxs[k], idx)
        gate = jnp.where(lane == k, es[k] / tot, gate)
    idx_ref[...] = idx
    gate_ref[...] = gate


def _router(h, w_r, b_r):
    M, D = h.shape
    tm = _row_tile(M, 512)
    w = jnp.pad(w_r, ((0, 0), (0, LANES - N_EXPERTS)))
    b = jnp.pad(b_r, (0, LANES - N_EXPERTS)).reshape(1, LANES)
    row = pl.BlockSpec((tm, LANES), lambda i: (i, 0))
    return pl.pallas_call(
        _router_kernel,
        grid=(M // tm,),
        in_specs=[pl.BlockSpec((tm, D), lambda i: (i, 0)),
                  pl.BlockSpec((D, LANES), lambda i: (0, 0)),
                  pl.BlockSpec((1, LANES), lambda i: (0, 0))],
        out_specs=[row, row],
        out_shape=[jax.ShapeDtypeStruct((M, LANES), F32), jax.ShapeDtypeStruct((M, LANES), F32)],
        compiler_params=_cparams(("arbitrary",)),
        name="moe_router",
    )(h, w, b)


def _slot_kernel(idx_ref, dest_ref, cnt_ref, run_ref, start_ref):
    phase, t = pl.program_id(0), pl.program_id(1)
    idx = idx_ref[...]
    rows = idx.shape[0]
    lane = _iota((rows, LANES), 1).astype(F32)
    onehots = [jnp.where(lane == _lane_pick(idx, k), 1.0, 0.0) for k in range(TOP_K)]

    @pl.when((phase == 0) & (t == 0))
    def _():
        run_ref[...] = jnp.zeros_like(run_ref)

    @pl.when(phase == 0)
    def _():
        tot = onehots[0] + onehots[1] + onehots[2] + onehots[3]
        run_ref[...] += jnp.sum(tot, axis=0, keepdims=True)

    @pl.when((phase == 1) & (t == 0))
    def _():
        counts = run_ref[...]
        cnt_ref[...] = jnp.broadcast_to(counts, cnt_ref.shape)
        padded = jnp.floor((counts + (MOE_TILE - 1)) * (1.0 / MOE_TILE)) * MOE_TILE
        hi, mid, lo = _split3(jnp.broadcast_to(padded, (8, LANES)))
        tri = _tri(LANES, "lt")
        start_ref[...] = (_dot(hi, tri) + _dot(mid, tri) + _dot(lo, tri))[0:1]
        run_ref[...] = jnp.zeros_like(run_ref)

    @pl.when(phase == 1)
    def _():
        tri = _tri(rows, "ge")
        run = run_ref[...]
        out = jnp.zeros((rows, LANES), F32)
        for k, oh in enumerate(onehots):
            cum = _dot(tri, oh.astype(BF16)) + run
            slot = jnp.sum(oh * (start_ref[...] + cum - 1.0), axis=-1, keepdims=True)
            out = jnp.where(lane == k, slot, out)
            run = run + jnp.sum(oh, axis=0, keepdims=True)
        run_ref[...] = run
        dest_ref[...] = out.astype(jnp.int32)


def _slots(idx):
    M = idx.shape[0]
    tm = _row_tile(M, 512)
    return pl.pallas_call(
        _slot_kernel,
        grid=(2, M // tm),
        in_specs=[pl.BlockSpec((tm, LANES), lambda p, t: (t, 0))],
        out_specs=[pl.BlockSpec((tm, LANES), lambda p, t: (t * p, 0)),
                   pl.BlockSpec((8, LANES), lambda p, t: (0, 0))],
        out_shape=[jax.ShapeDtypeStruct((M, LANES), jnp.int32), jax.ShapeDtypeStruct((8, LANES), F32)],
        scratch_shapes=[pltpu.VMEM((1, LANES), F32), pltpu.VMEM((1, LANES), F32)],
        compiler_params=_cparams(("arbitrary", "arbitrary")),
        name="moe_slots",
    )(idx)


def _expert_kernel(te_ref, nt_ref, x_ref, wgu_ref, bgu_ref, wdn_ref, bdn_ref, o_ref, wgu_bf, wdn_bf):
    t = pl.program_id(0)

    @pl.when(t < nt_ref[0])
    def _():
        prev = te_ref[jnp.maximum(t - 1, 0)]

        @pl.when((t == 0) | (te_ref[t] != prev))
        def _():
            wgu_bf[...] = wgu_ref[0, 0].astype(BF16)
            wdn_bf[...] = wdn_ref[0, 0].astype(BF16)

        gu = _dot(x_ref[...].astype(BF16), wgu_bf[...]) + bgu_ref[0, 0]
        gate = jnp.minimum(gu[:, :D_EXPERT], SWIGLU_LIMIT)
        up = jnp.clip(gu[:, D_EXPERT:], -SWIGLU_LIMIT, SWIGLU_LIMIT)
        act = (up + 1.0) * gate * jax.nn.sigmoid(gate * SWIGLU_ALPHA)
        o_ref[...] = _dot(act.astype(BF16), wdn_bf[...]) + bdn_ref[0, 0]

    @pl.when(t >= nt_ref[0])
    def _():
        o_ref[...] = jnp.zeros_like(o_ref)


def _moe(h, layer, w_r, b_r, w_gu, b_gu, w_dn, b_dn):
    n, d = h.shape
    tm = MOE_TILE
    idx, gates = _router(h, w_r, b_r)
    dest, counts = _slots(idx)
    dest = dest[:, :TOP_K]
    counts = counts[0, :N_EXPERTS].astype(jnp.int32)
    ends = jnp.cumsum(((counts + tm - 1) // tm) * tm)
    n_tiles = (n * TOP_K + N_EXPERTS * (tm - 1)) // tm + 1
    slots = n_tiles * tm
    used = (ends[-1] // tm).astype(jnp.int32)
    tile_start = jnp.minimum(jnp.arange(n_tiles, dtype=jnp.int32), used - 1) * tm
    tile_e = jnp.sum((ends[None, :] <= tile_start[:, None]).astype(jnp.int32), axis=1)
    token = jnp.zeros((slots,), jnp.int32).at[dest.reshape(-1)].set(
        jnp.arange(n * TOP_K, dtype=jnp.int32) // TOP_K, mode="promise_in_bounds")
    xs = h.at[token].get(mode="promise_in_bounds")

    tile = lambda t, te, nt: (jnp.minimum(t, nt[0] - 1), 0)
    expert = lambda t, te, nt: (layer, te[t], 0, 0)
    y = pl.pallas_call(
        _expert_kernel,
        grid_spec=pltpu.PrefetchScalarGridSpec(
            num_scalar_prefetch=2,
            grid=(n_tiles,),
            in_specs=[pl.BlockSpec((tm, d), tile),
                      pl.BlockSpec((1, 1, d, 2 * D_EXPERT), expert),
                      pl.BlockSpec((1, 1, 1, 2 * D_EXPERT), expert),
                      pl.BlockSpec((1, 1, D_EXPERT, d), expert),
                      pl.BlockSpec((1, 1, 1, d), expert)],
            out_specs=pl.BlockSpec((tm, d), lambda t, te, nt: (t, 0)),
            scratch_shapes=[pltpu.VMEM((d, 2 * D_EXPERT), BF16), pltpu.VMEM((D_EXPERT, d), BF16)]),
        out_shape=jax.ShapeDtypeStruct((slots, d), F32),
        compiler_params=_cparams(("arbitrary",), VMEM_LIMIT),
        name="moe_experts",
    )(tile_e, used.reshape(1), xs, w_gu, b_gu.reshape(b_gu.shape[:2] + (1, -1)),
      w_dn, b_dn.reshape(b_dn.shape[:2] + (1, -1)))
    ys = [y.at[dest[:, k]].get(mode="promise_in_bounds") for k in range(TOP_K)]
    return ys, gates


def kernel(x_prompt, x_sample, cache_sb_k, cache_sb_v, cache_fox_k, cache_fox_v, cache_fox_logf,
           cache_mla_ckv, cache_mla_krope, meta_tokens, a_w_qkv, a_w_o, b_w_qkv, b_w_f, b_b_f, b_w_o,
           c_w_dq, c_g_q, c_w_uq, c_w_dkv, c_g_kv, c_w_uk, c_w_uv, c_w_o, ln1_g, ln1_b, ln2_g, ln2_b,
           moe_w_router, moe_b_router, moe_w_gate_up, moe_b_gate_up, moe_w_down, moe_b_down):
    bp, sp, d = x_prompt.shape
    bs, ls, _ = x_sample.shape
    lp = N_META + sp
    past = cache_sb_k.shape[2]
    dims = (bp, lp, bs, ls, past)
    np_ = bp * lp
    hd = N_HEADS * HEAD_DIM

    meta = jnp.broadcast_to(meta_tokens[None], (bp, N_META, d))
    h = jnp.concatenate([jnp.concatenate([meta, x_prompt], axis=1).reshape(np_, d),
                         x_sample.reshape(bs * ls, d)], axis=0)
    hb = h.astype(BF16)
    new = {k: [] for k in ("a_k", "a_v", "b_k", "b_v", "b_lf", "c_c", "c_kr")}

    def heads(x):
        return x.reshape(-1, N_HEADS, HEAD_DIM)

    sb_k, sb_v, fox_k, fox_v = (c.reshape(c.shape[:3] + (hd,))
                                for c in (cache_sb_k, cache_sb_v, cache_fox_k, cache_fox_v))

    for i in range(DEPTH):
        kind, j = i % 3, i // 3
        if kind == 0:
            qkv = _qkv_proj(hb, a_w_qkv[j], "sb_qkv_proj")
            o = _sb_attention(qkv, sb_k, sb_v, j, dims)
            y = _matmul(o, a_w_o[j].astype(BF16), name="sb_out_proj")
            new["a_k"].append(heads(qkv[1]))
            new["a_v"].append(heads(qkv[2]))
        elif kind == 1:
            qkv = _qkv_proj(hb, b_w_qkv[j], "fox_qkv_proj")
            o, lf = _fox_attention(hb, qkv, b_w_f[j], b_b_f[j], fox_k, fox_v, cache_fox_logf[j], j, dims)
            y = _matmul(o, b_w_o[j].astype(BF16), name="fox_out_proj")
            new["b_k"].append(heads(qkv[1]))
            new["b_v"].append(heads(qkv[2]))
            new["b_lf"].append(lf)
        else:
            o, c, kr = _mla_attention(hb, (c_w_dq[j], c_g_q[j], c_w_uq[j], c_w_dkv[j], c_g_kv[j],
                                           c_w_uk[j], c_w_uv[j]), cache_mla_ckv[j], cache_mla_krope[j], dims)
            y = _matmul(o, c_w_o[j].astype(BF16), name="mla_out_proj")
            new["c_c"].append(c)
            new["c_kr"].append(kr)
        h, hb = _deepnorm_ln(h, [y], None, ln1_g[i], ln1_b[i], name="ln_mixer")
        ys, gates = _moe(h, i, moe_w_router[i], moe_b_router[i], moe_w_gate_up, moe_b_gate_up,
                         moe_w_down, moe_b_down)
        h, hb = _deepnorm_ln(h, ys, gates, ln2_g[i], ln2_b[i], name="ln_moe")

    def split(rows, tail_shape):
        xs = jnp.stack(rows)
        return (xs[:, :np_].reshape((len(rows), bp, lp) + tail_shape),
                xs[:, np_:].reshape((len(rows), bs, ls) + tail_shape))

    outs_p, outs_s = [], []
    for key, tail_shape in (("a_k", (N_HEADS, HEAD_DIM)), ("a_v", (N_HEADS, HEAD_DIM)),
                            ("b_k", (N_HEADS, HEAD_DIM)), ("b_v", (N_HEADS, HEAD_DIM)),
                            ("b_lf", (N_HEADS,)), ("c_c", (KV_LORA,)), ("c_kr", (QK_ROPE,))):
        p, s = split(new[key], tail_shape)
        outs_p.append(p)
        outs_s.append(s)
    y_prompt = h[:np_].reshape(bp, lp, d)[:, N_META:]
    y_sample = h[np_:].reshape(bs, ls, d)
    return (y_prompt, y_sample) + tuple(outs_p) + tuple(outs_s)
```
